```python
import jax, jax.numpy as jnp
from jax import lax
import numpy as np

D_MODEL = 1024
BATCH = 8
SEQ = 2048
DEPTH = 1

N_MEM = 256
GDN_HEADS = 4
GDN_HEAD_DIM = 128
GDN_CONV = 4
GDN_CHUNK = 64
SWA_HEADS = 8
SWA_HEAD_DIM = 64
DILATED_BRANCHES = ((128, 1), (512, 4), (2048, 16))
SWA_BLOCK = 128
MEM_HEADS = 4
MEM_HEAD_DIM = D_MODEL // MEM_HEADS
D_FF = 2816
EPS = 1e-6

GDN_W = GDN_HEADS * GDN_HEAD_DIM
SWA_W = SWA_HEADS * SWA_HEAD_DIM
MIX_W = GDN_W + SWA_W
IN_SIZES = (GDN_W, GDN_W, GDN_W, GDN_W, GDN_HEADS, GDN_HEADS, SWA_W, SWA_W, SWA_W)
IN_COLS = sum(IN_SIZES)
IN_SPLITS = tuple(int(c) for c in np.cumsum(IN_SIZES)[:-1])

kernel_name = "hybrid_gdn_dilated_alibi_macaron_sandwich"


def rms_norm(x, g):
    xf = x.astype(jnp.float32)
    y = xf * lax.rsqrt(jnp.mean(xf * xf, axis=-1, keepdims=True) + EPS)
    return (y * g.astype(jnp.float32)).astype(x.dtype)


def swiglu(x, w_gate, w_up, w_down):
    return (jax.nn.silu(x @ w_gate) * (x @ w_up)) @ w_down


def l2_normalize(x):
    return x * lax.rsqrt(jnp.sum(x * x, axis=-1, keepdims=True) + EPS)


def causal_depthwise_conv(x, w):
    c = x.shape[-1]
    return lax.conv_general_dilated(
        x, w[:, None, :].astype(x.dtype), window_strides=(1,),
        padding=((GDN_CONV - 1, 0),), dimension_numbers=("NWC", "WIO", "NWC"),
        feature_group_count=c)


def gated_delta_chunked(q, k, v, g, beta):
    b_, h_, s_, dk = q.shape
    dv = v.shape[-1]
    c = GDN_CHUNK
    n = s_ // c
    q = q.reshape(b_, h_, n, c, dk)
    k = k.reshape(b_, h_, n, c, dk)
    v = v.reshape(b_, h_, n, c, dv)
    g = jnp.cumsum(g.reshape(b_, h_, n, c), axis=-1)
    beta = beta.reshape(b_, h_, n, c)
    idx = jnp.arange(c)
    tril = idx[:, None] >= idx[None, :]
    strict = idx[:, None] > idx[None, :]
    decay = jnp.exp(jnp.where(tril, g[..., :, None] - g[..., None, :], -jnp.inf))
    k_beta = k * beta[..., None]
    v_beta = v * beta[..., None]
    a = jnp.where(strict, jnp.einsum("bhncd,bhnsd->bhncs", k_beta, k) * decay, 0.0)
    eye = jnp.eye(c, dtype=q.dtype)
    t_mat = lax.linalg.triangular_solve(eye + a, jnp.broadcast_to(eye, a.shape),
                                        left_side=True, lower=True, unit_diagonal=True)
    u = t_mat @ v_beta
    w = t_mat @ (k_beta * jnp.exp(g)[..., None])
    qk = jnp.where(tril, jnp.einsum("bhncd,bhnsd->bhncs", q, k) * decay, 0.0)
    q_dec = q * jnp.exp(g)[..., None]
    k_dec = k * jnp.exp(g[..., -1:] - g)[..., None]
    g_last = jnp.exp(g[..., -1])

    def step(state, xs):
        u_c, w_c, qk_c, qd_c, kd_c, gl_c = xs
        v_new = u_c - w_c @ state
        o_c = qd_c @ state + qk_c @ v_new
        state = state * gl_c[..., None, None] + jnp.einsum("bhcd,bhce->bhde", kd_c, v_new)
        return state, o_c

    xs = tuple(jnp.moveaxis(t, 2, 0) for t in (u, w, qk, q_dec, k_dec, g_last))
    state0 = jnp.zeros((b_, h_, dk, dv), q.dtype)
    _, o = lax.scan(step, state0, xs)
    return jnp.moveaxis(o, 0, 2).reshape(b_, h_, s_, dv)


def gdn_mixer(q, k, v, z, a, b, conv_w, a_log, dt_bias, norm_g):
    b_, s_, _ = q.shape
    qkv = jax.nn.silu(causal_depthwise_conv(jnp.concatenate([q, k, v], axis=-1), conv_w))
    q, k, v = jnp.split(qkv.astype(jnp.float32), 3, axis=-1)
    heads = lambda t: t.reshape(b_, s_, GDN_HEADS, GDN_HEAD_DIM).transpose(0, 2, 1, 3)
    q = l2_normalize(heads(q)) * (GDN_HEAD_DIM ** -0.5)
    k = l2_normalize(heads(k))
    v = heads(v)
    beta = jax.nn.sigmoid(b.astype(jnp.float32)).transpose(0, 2, 1)
    g = -jnp.exp(a_log.astype(jnp.float32)) * jax.nn.softplus(
        a.astype(jnp.float32) + dt_bias.astype(jnp.float32))
    g = g.transpose(0, 2, 1)
    o = gated_delta_chunked(q, k, v, g, beta).transpose(0, 2, 1, 3)
    zf = z.astype(jnp.float32).reshape(b_, s_, GDN_HEADS, GDN_HEAD_DIM)
    o = o * lax.rsqrt(jnp.mean(o * o, axis=-1, keepdims=True) + EPS)
    o = o * norm_g.astype(jnp.float32) * jax.nn.silu(zf)
    return o.reshape(b_, s_, GDN_W).astype(q.dtype if False else z.dtype)


def dilated_branch(q, k, v, slopes, window, dilation):
    b_, h_, s_, dh = q.shape
    d = dilation
    n_back = window // dilation
    sub_len = s_ // d
    nb = -(-sub_len // SWA_BLOCK)
    lp = nb * SWA_BLOCK

    def gather_stride(t):
        t = t.reshape(b_, h_, sub_len, d, dh).transpose(0, 1, 3, 2, 4)
        return jnp.pad(t, ((0, 0), (0, 0), (0, 0), (0, lp - sub_len), (0, 0)))

    def kv_band(t):
        tb = t.reshape(b_, h_, d, nb, SWA_BLOCK, dh)
        prev = jnp.pad(tb, ((0, 0), (0, 0), (0, 0), (1, 0), (0, 0), (0, 0)))[:, :, :, :-1]
        return jnp.concatenate([prev, tb], axis=4)

    qb = gather_stride(q).reshape(b_, h_, d, nb, SWA_BLOCK, dh)
    kb = kv_band(gather_stride(k))
    vb = kv_band(gather_stride(v))
    s = jnp.einsum("bhrnqd,bhrnkd->bhrnqk", qb, kb) * (dh ** -0.5)
    p_idx = jnp.arange(SWA_BLOCK)
    c_idx = jnp.arange(2 * SWA_BLOCK)
    blk = jnp.arange(nb)
    delta = p_idx[:, None] + SWA_BLOCK - c_idx[None, :]
    key_pos = blk[:, None] * SWA_BLOCK + c_idx[None, :] - SWA_BLOCK
    valid = (delta >= 0)[None] & (delta <= n_back)[None] & (key_pos[:, None, :] >= 0)
    alibi = -slopes[:, None, None, None, None] * (dilation * delta).astype(jnp.float32)
    s = jnp.where(valid, s + alibi, -jnp.inf)
    m = jnp.max(s, axis=-1, keepdims=True)
    e = jnp.exp(s - m)
    l = jnp.sum(e, axis=-1, keepdims=True)
    o = jnp.einsum("bhrnqk,bhrnkd->bhrnqd", e, vb) / l
    lse = (m + jnp.log(l))[..., 0]
    o = o.reshape(b_, h_, d, lp, dh)[:, :, :, :sub_len].transpose(0, 1, 3, 2, 4).reshape(b_, h_, s_, dh)
    lse = lse.reshape(b_, h_, d, lp)[:, :, :, :sub_len].transpose(0, 1, 3, 2).reshape(b_, h_, s_)
    return o, lse


def dilated_mixer(q, k, v):
    b_, s_, _ = q.shape
    heads = lambda t: t.reshape(b_, s_, SWA_HEADS, SWA_HEAD_DIM).transpose(0, 2, 1, 3).astype(jnp.float32)
    qh, kh, vh = heads(q), heads(k), heads(v)
    slopes = 2.0 ** (-8.0 * jnp.arange(1, SWA_HEADS + 1, dtype=jnp.float32) / SWA_HEADS)
    outs, lses = [], []
    for window, dilation in DILATED_BRANCHES:
        o_i, lse_i = dilated_branch(qh, kh, vh, slopes, window, dilation)
        outs.append(o_i)
        lses.append(lse_i)
    wts = jax.nn.softmax(jnp.stack(lses, axis=0), axis=0)
    o = jnp.einsum("gbhs,gbhsd->bhsd", wts, jnp.stack(outs, axis=0))
    return o.transpose(0, 2, 1, 3).reshape(b_, s_, SWA_W).astype(q.dtype)


def memory_cross_attention(h, mem_n, wq, wk, wv, wo):
    b_, s_, _ = h.shape
    m_ = mem_n.shape[1]
    q = (h @ wq).reshape(b_, s_, MEM_HEADS, MEM_HEAD_DIM)
    k = (mem_n @ wk).reshape(b_, m_, MEM_HEADS, MEM_HEAD_DIM)
    v = (mem_n @ wv).reshape(b_, m_, MEM_HEADS, MEM_HEAD_DIM)
    s = jnp.einsum("bshd,bmhd->bhsm", q, k).astype(jnp.float32) * (MEM_HEAD_DIM ** -0.5)
    p = jax.nn.softmax(s, axis=-1).astype(v.dtype)
    o = jnp.einsum("bhsm,bmhd->bshd", p, v).reshape(b_, s_, D_MODEL)
    return o @ wo


def setup_inputs(seed: int = 0) -> dict:
    key = jax.random.key(seed)
    ks = iter(jax.random.split(key, 40))
    nrm = lambda shape, fan_in: jax.random.normal(next(ks), shape, jnp.float32) * (fan_in ** -0.5)
    gain = lambda n: 1.0 + 0.05 * jax.random.normal(next(ks), (DEPTH, n), jnp.float32)
    x = jax.random.normal(next(ks), (BATCH, SEQ, D_MODEL), jnp.float32)
    mem = jax.random.normal(next(ks), (BATCH, N_MEM, D_MODEL), jnp.float32)
    a_log = jnp.log(jax.random.uniform(next(ks), (DEPTH, GDN_HEADS), jnp.float32, 1.0, 16.0))
    dt = jnp.exp(jax.random.uniform(next(ks), (DEPTH, GDN_HEADS), jnp.float32,
                                    np.log(1e-3), np.log(1e-1)))
    dt_bias = dt + jnp.log(-jnp.expm1(-dt))
    return {
        "x": x,
        "mem": mem,
        "ffn1_pre_g": gain(D_MODEL),
        "ffn1_w_gate": nrm((DEPTH, D_MODEL, D_FF), D_MODEL),
        "ffn1_w_up": nrm((DEPTH, D_MODEL, D_FF), D_MODEL),
        "ffn1_w_down": nrm((DEPTH, D_FF, D_MODEL), D_FF),
        "ffn1_post_g": gain(D_MODEL),
        "mix_pre_g": gain(D_MODEL),
        "w_in": nrm((DEPTH, D_MODEL, IN_COLS), D_MODEL),
        "gdn_conv_w": nrm((DEPTH, GDN_CONV, 3 * GDN_W), GDN_CONV),
        "gdn_a_log": a_log,
        "gdn_dt_bias": dt_bias,
        "gdn_norm_g": gain(GDN_HEAD_DIM),
        "w_out": nrm((DEPTH, MIX_W, D_MODEL), MIX_W),
        "mix_post_g": gain(D_MODEL),
        "mem_pre_g": gain(D_MODEL),
        "mem_kv_g": gain(D_MODEL),
        "mem_wq": nrm((DEPTH, D_MODEL, D_MODEL), D_MODEL),
        "mem_wk": nrm((DEPTH, D_MODEL, D_MODEL), D_MODEL),
        "mem_wv": nrm((DEPTH, D_MODEL, D_MODEL), D_MODEL),
        "mem_wo": nrm((DEPTH, D_MODEL, D_MODEL), D_MODEL),
        "mem_post_g": gain(D_MODEL),
        "ffn2_pre_g": gain(D_MODEL),
        "ffn2_w_gate": nrm((DEPTH, D_MODEL, D_FF), D_MODEL),
        "ffn2_w_up": nrm((DEPTH, D_MODEL, D_FF), D_MODEL),
        "ffn2_w_down": nrm((DEPTH, D_FF, D_MODEL), D_FF),
        "ffn2_post_g": gain(D_MODEL),
    }


def reference(x, mem, ffn1_pre_g, ffn1_w_gate, ffn1_w_up, ffn1_w_down, ffn1_post_g,
              mix_pre_g, w_in, gdn_conv_w, gdn_a_log, gdn_dt_bias, gdn_norm_g, w_out, mix_post_g,
              mem_pre_g, mem_kv_g, mem_wq, mem_wk, mem_wv, mem_wo, mem_post_g,
              ffn2_pre_g, ffn2_w_gate, ffn2_w_up, ffn2_w_down, ffn2_post_g):
    for l in range(DEPTH):
        f = swiglu(rms_norm(x, ffn1_pre_g[l]), ffn1_w_gate[l], ffn1_w_up[l], ffn1_w_down[l])
        x = x + 0.5 * rms_norm(f, ffn1_post_g[l])
        h = rms_norm(x, mix_pre_g[l])
        proj = h @ w_in[l]
        qa, ka, va, za, aa, ba, qb, kb, vb = jnp.split(proj, IN_SPLITS, axis=-1)
        o_a = gdn_mixer(qa, ka, va, za, aa, ba, gdn_conv_w[l], gdn_a_log[l], gdn_dt_bias[l], gdn_norm_g[l])
        o_b = dilated_mixer(qb, kb, vb)
        mix = jnp.concatenate([o_a, o_b], axis=-1) @ w_out[l]
        x = x + rms_norm(mix, mix_post_g[l])
        c = memory_cross_attention(rms_norm(x, mem_pre_g[l]), rms_norm(mem, mem_kv_g[l]),
                                   mem_wq[l], mem_wk[l], mem_wv[l], mem_wo[l])
        x = x + rms_norm(c, mem_post_g[l])
        f = swiglu(rms_norm(x, ffn2_pre_g[l]), ffn2_w_gate[l], ffn2_w_up[l], ffn2_w_down[l])
        x = x + 0.5 * rms_norm(f, ffn2_post_g[l])
    return x
```

```python
import functools

import jax
import jax.numpy as jnp
from jax import lax
from jax.experimental import pallas as pl
from jax.experimental.pallas import tpu as pltpu

F32 = jnp.float32
BF16 = jnp.bfloat16

D_MODEL = 1024
D_FF = 2816
EPS = 1e-6
GDN_HEADS = 4
GDN_DIM = 128
GDN_W = GDN_HEADS * GDN_DIM
GDN_CONV = 4
CHUNK = 64
SWA_HEADS = 8
SWA_DIM = 64
SWA_W = SWA_HEADS * SWA_DIM
SWA_BLOCK = 128
DILATIONS = (1, 4, 16)
MEM_HEADS = 4
MEM_DIM = D_MODEL // MEM_HEADS
NEG = -1e30

V7X_VMEM_BYTES = 64 * 1024 * 1024
VMEM_LIMIT = V7X_VMEM_BYTES - 8 * 1024 * 1024

TM = 512
FFN_FC = 256
GDN_TS = 256


def _rms(x, g):
    return x * lax.rsqrt(jnp.mean(x * x, axis=-1, keepdims=True) + EPS) * g


def _dot(a, b):
    return jnp.dot(a, b, preferred_element_type=F32)


def _dot_nt(a, b):
    return lax.dot_general(a, b, (((1,), (1,)), ((), ())), preferred_element_type=F32)


def _dot_tn(a, b):
    return lax.dot_general(a, b, (((0,), (0,)), ((), ())), preferred_element_type=F32)


def _resident(shape):
    return pl.BlockSpec(shape, lambda *_: (0,) * len(shape), pipeline_mode=pl.Buffered(1))


def _params(*sem):
    return pltpu.CompilerParams(dimension_semantics=sem, vmem_limit_bytes=VMEM_LIMIT)


def _ffn_kernel(x_ref, gpre_ref, wg_ref, wu_ref, wd_ref, gpost_ref, o_ref, a_ref):
    x = x_ref[...]
    xn = _rms(x, gpre_ref[...]).astype(BF16)
    for c in range(D_FF // FFN_FC):
        sl = slice(c * FFN_FC, (c + 1) * FFN_FC)
        g = _dot(xn, wg_ref[:, sl])
        u = _dot(xn, wu_ref[:, sl])
        a_ref[:, sl] = (g * jax.nn.sigmoid(g) * u).astype(BF16)
    f = _dot(a_ref[...], wd_ref[...])
    o_ref[...] = x + 0.5 * _rms(f, gpost_ref[...])


def _ffn(x, gpre, wg, wu, wd, gpost):
    t = x.shape[0]
    row = pl.BlockSpec((TM, D_MODEL), lambda i: (i, 0))
    return pl.pallas_call(
        _ffn_kernel,
        grid=(t // TM,),
        in_specs=[row, _resident((1, D_MODEL)), _resident((D_MODEL, D_FF)),
                  _resident((D_MODEL, D_FF)), _resident((D_FF, D_MODEL)),
                  _resident((1, D_MODEL))],
        out_specs=row,
        out_shape=jax.ShapeDtypeStruct((t, D_MODEL), F32),
        scratch_shapes=[pltpu.VMEM((TM, D_FF), BF16)],
        compiler_params=_params("parallel"),
        name="ffn",
    )(x, gpre, wg, wu, wd, gpost)


def _proj_kernel(x_ref, g_ref, wa_ref, wz_ref, wab_ref, wabt_ref, wb_ref,
                 qkva_ref, z_ref, abc_ref, abr_ref, qkvb_ref):
    h = _rms(x_ref[...], g_ref[...]).astype(BF16)
    qkva_ref[...] = _dot(h, wa_ref[...])
    z_ref[...] = _dot(h, wz_ref[...])
    abc_ref[...] = _dot(h, wab_ref[...])
    abr_ref[...] = _dot_nt(wabt_ref[...], h)
    qkvb_ref[...] = _dot(h, wb_ref[...])


def _proj(x, g, wa, wz, wab, wabt, wb):
    t = x.shape[0]
    row = lambda w: pl.BlockSpec((TM, w), lambda i: (i, 0))
    return pl.pallas_call(
        _proj_kernel,
        grid=(t // TM,),
        in_specs=[row(D_MODEL), _resident((1, D_MODEL)), _resident(wa.shape),
                  _resident(wz.shape), _resident(wab.shape), _resident(wabt.shape),
                  _resident(wb.shape)],
        out_specs=[row(3 * GDN_W), row(GDN_W), row(128),
                   pl.BlockSpec((16, TM), lambda i: (0, i)), row(3 * SWA_W)],
        out_shape=[jax.ShapeDtypeStruct((t, 3 * GDN_W), F32),
                   jax.ShapeDtypeStruct((t, GDN_W), F32),
                   jax.ShapeDtypeStruct((t, 128), F32),
                   jax.ShapeDtypeStruct((16, t), F32),
                   jax.ShapeDtypeStruct((t, 3 * SWA_W), F32)],
        compiler_params=_params("parallel"),
        name="mixer_proj",
    )(x, g, wa, wz, wab, wabt, wb)


def _softplus(x):
    return jnp.maximum(x, 0.0) + jnp.log(1.0 + jnp.exp(-jnp.abs(x)))


def _unit_lower_inverse(a):
    row = lax.broadcasted_iota(jnp.int32, (CHUNK, CHUNK), 0)
    col = lax.broadcasted_iota(jnp.int32, (CHUNK, CHUNK), 1)
    t = jnp.where(row == col, 1.0, 0.0).astype(F32)
    b = 1
    while b < CHUNK:
        lb = b.bit_length() - 1
        sel = (((row >> (lb + 1)) == (col >> (lb + 1))) & (((row >> lb) & 1) == 1)
               & (((col >> lb) & 1) == 0))
        ab = jnp.where(sel, a, 0.0)
        if b == 1:
            t = t - ab
        else:
            tb = t.astype(BF16)
            p = _dot(ab.astype(BF16), tb)
            t = t - _dot(tb, p.astype(BF16))
        b *= 2
    return t


def _gdn_kernel(x_ref, z_ref, abc_ref, abr_ref, cw_ref, alc_ref, dtc_ref, alr_ref, dtr_ref,
                ng_ref, o_ref,
                xpad, st, qn, kn, vn, gcs, us, ws, qds, kds, qks):
    ts = GDN_TS
    nchunk = ts // CHUNK
    t_idx = pl.program_id(1)

    @pl.when(t_idx == 0)
    def _():
        xpad[0:8, :] = jnp.zeros((8, 3 * GDN_W), F32)
        st[...] = jnp.zeros_like(st)

    xpad[8:8 + ts, :] = x_ref[...]
    for c in range(nchunk):
        y = jnp.zeros((CHUNK, 3 * GDN_W), F32)
        for j in range(GDN_CONV):
            y = y + cw_ref[j:j + 1, :] * xpad[c * CHUNK + 5 + j:c * CHUNK + 5 + j + CHUNK, :]
        y = y * jax.nn.sigmoid(y)
        rows = slice(c * CHUNK, (c + 1) * CHUNK)
        for h in range(GDN_HEADS):
            ql = slice(h * GDN_DIM, (h + 1) * GDN_DIM)
            kl = slice(GDN_W + h * GDN_DIM, GDN_W + (h + 1) * GDN_DIM)
            q = y[:, ql]
            k = y[:, kl]
            qn[rows, ql] = q * lax.rsqrt(jnp.sum(q * q, axis=-1, keepdims=True) + EPS) * (GDN_DIM ** -0.5)
            kn[rows, ql] = k * lax.rsqrt(jnp.sum(k * k, axis=-1, keepdims=True) + EPS)
        vn[rows, :] = y[:, 2 * GDN_W:]
    xpad[0:8, :] = xpad[ts:ts + 8, :]

    abc = abc_ref[...]
    abr = abr_ref[...]
    g_col = -jnp.exp(alc_ref[...]) * _softplus(abc + dtc_ref[...])
    g_row = -jnp.exp(alr_ref[...]) * _softplus(abr + dtr_ref[...])
    ri = lax.broadcasted_iota(jnp.int32, (ts, ts), 0)
    ci = lax.broadcasted_iota(jnp.int32, (ts, ts), 1)
    lc = CHUNK.bit_length() - 1
    same = (ri >> lc) == (ci >> lc)
    lower = jnp.where(same & (ci <= ri), 1.0, 0.0).astype(F32)
    upper = jnp.where(same & (ri <= ci), 1.0, 0.0).astype(F32)
    gc_col = jnp.dot(lower, g_col, precision=lax.Precision.HIGHEST, preferred_element_type=F32)
    gc_row = jnp.dot(g_row, upper, precision=lax.Precision.HIGHEST, preferred_element_type=F32)
    gcs[...] = gc_col
    beta_all = jax.nn.sigmoid(abc)

    r64 = lax.broadcasted_iota(jnp.int32, (CHUNK, CHUNK), 0)
    c64 = lax.broadcasted_iota(jnp.int32, (CHUNK, CHUNK), 1)
    tril = c64 <= r64
    strict = c64 < r64

    for c in range(nchunk):
        rows = slice(c * CHUNK, (c + 1) * CHUNK)
        for h in range(GDN_HEADS):
            hl = slice(h * GDN_DIM, (h + 1) * GDN_DIM)
            q = qn[rows, hl]
            k = kn[rows, hl]
            v = vn[rows, hl]
            gcc = gc_col[c * CHUNK:(c + 1) * CHUNK, h:h + 1]
            gcr = gc_row[h:h + 1, c * CHUNK:(c + 1) * CHUNK]
            gl = gc_col[(c + 1) * CHUNK - 1:(c + 1) * CHUNK, h:h + 1]
            beta = beta_all[c * CHUNK:(c + 1) * CHUNK, 4 + h:5 + h]
            decay = jnp.exp(jnp.where(tril, gcc - gcr, NEG))
            eg = jnp.exp(gcc)
            kb = k * beta
            kq = _dot_nt(jnp.concatenate([kb, q], axis=0).astype(BF16), k.astype(BF16))
            a = jnp.where(strict, kq[:CHUNK] * decay, 0.0)
            qk = jnp.where(tril, kq[CHUNK:] * decay, 0.0)
            t = _unit_lower_inverse(a)
            rhs = jnp.concatenate([v * beta, kb * eg], axis=1).astype(BF16)
            uw = _dot(t.astype(BF16), rhs)
            us[rows, hl] = uw[:, :GDN_DIM]
            ws[rows, hl] = uw[:, GDN_DIM:].astype(BF16)
            qds[rows, hl] = (q * eg).astype(BF16)
            kds[rows, hl] = (k * jnp.exp(gl - gcc)).astype(BF16)
            qks[rows, hl] = jnp.concatenate([qk, jnp.zeros_like(qk)], axis=1).astype(BF16)

    def step(c, carry):
        r0 = pl.multiple_of(c * CHUNK, CHUNK)
        rows = pl.ds(r0, CHUNK)
        g8 = gcs[pl.ds(pl.multiple_of(r0 + CHUNK - 8, 8), 8), :]
        for h in range(GDN_HEADS):
            hl = slice(h * GDN_DIM, (h + 1) * GDN_DIM)
            s = st[h]
            sb = s.astype(BF16)
            v_new = us[rows, hl] - _dot(ws[rows, hl], sb)
            vb = v_new.astype(BF16)
            o = _dot(qds[rows, hl], sb) + _dot(qks[rows, hl][:, :CHUNK], vb)
            st[h] = s * jnp.exp(g8[7:8, h:h + 1]) + _dot_tn(kds[rows, hl], vb)
            zz = z_ref[rows, hl]
            o = o * lax.rsqrt(jnp.mean(o * o, axis=-1, keepdims=True) + EPS)
            o_ref[rows, hl] = o * ng_ref[...] * (zz * jax.nn.sigmoid(zz))
        return carry

    lax.fori_loop(0, nchunk, step, 0)


def _gdn(qkva, z, abc, abr, cw, alc, dtc, alr, dtr, ng, batch, seq):
    ts = GDN_TS
    nt = seq // ts
    row = lambda w: pl.BlockSpec((ts, w), lambda b, t: (b * nt + t, 0))
    const = lambda shape: pl.BlockSpec(shape, lambda b, t: (0, 0))
    return pl.pallas_call(
        _gdn_kernel,
        grid=(batch, nt),
        in_specs=[row(3 * GDN_W), row(GDN_W), row(128),
                  pl.BlockSpec((16, ts), lambda b, t: (0, b * nt + t)),
                  const((8, 3 * GDN_W)), const((1, 128)), const((1, 128)),
                  const((16, 1)), const((16, 1)), const((1, GDN_DIM))],
        out_specs=row(GDN_W),
        out_shape=jax.ShapeDtypeStruct((batch * seq, GDN_W), F32),
        scratch_shapes=[
            pltpu.VMEM((ts + 8, 3 * GDN_W), F32),
            pltpu.VMEM((GDN_HEADS, GDN_DIM, GDN_DIM), F32),
            pltpu.VMEM((ts, GDN_W), F32),
            pltpu.VMEM((ts, GDN_W), F32),
            pltpu.VMEM((ts, GDN_W), F32),
            pltpu.VMEM((ts, 128), F32),
            pltpu.VMEM((ts, GDN_W), F32),
            pltpu.VMEM((ts, GDN_W), BF16),
            pltpu.VMEM((ts, GDN_W), BF16),
            pltpu.VMEM((ts, GDN_W), BF16),
            pltpu.VMEM((ts, GDN_W), BF16),
        ],
        compiler_params=_params("parallel", "arbitrary"),
        name="gdn",
    )(qkva, z, abc, abr, cw, alc, dtc, alr, dtr, ng)


def _dil_kernel(q_ref, k_ref, v_ref, o_ref, qs, ks, vs, ores, lres, onat, lnat, *, seq):
    hp = pl.program_id(1)
    lane = lax.broadcasted_iota(jnp.int32, (1, 2 * SWA_DIM), 1)
    head_a = lane < SWA_DIM
    row = lax.broadcasted_iota(jnp.int32, (SWA_BLOCK, 2 * SWA_BLOCK), 0)
    col = lax.broadcasted_iota(jnp.int32, (SWA_BLOCK, 2 * SWA_BLOCK), 1)
    delta = row + SWA_BLOCK - col
    valid = (delta >= 0) & (delta <= SWA_BLOCK)
    valid_first = valid & (col >= SWA_BLOCK)
    delta_f = delta.astype(F32)

    def slope(head):
        e = jnp.full((SWA_BLOCK, 2 * SWA_BLOCK), 126, jnp.int32) - head
        return lax.bitcast_convert_type(lax.shift_left(e, jnp.full_like(e, 23)), F32)

    slope_a = slope(2 * hp)
    slope_b = slope(2 * hp + 1)

    for g, d in enumerate(DILATIONS):
        sub = seq // d
        nb = sub // SWA_BLOCK
        stride = sub + SWA_BLOCK
        for r in range(d):
            src = pl.ds(r, sub, stride=d) if d > 1 else pl.ds(0, sub)
            qs[r * sub:(r + 1) * sub, :] = (q_ref[src, :] * (SWA_DIM ** -0.5)).astype(BF16)
            ks[r * stride:r * stride + SWA_BLOCK, :] = jnp.zeros((SWA_BLOCK, 2 * SWA_DIM), BF16)
            vs[r * stride:r * stride + SWA_BLOCK, :] = jnp.zeros((SWA_BLOCK, 2 * SWA_DIM), BF16)
            ks[r * stride + SWA_BLOCK:(r + 1) * stride, :] = k_ref[src, :].astype(BF16)
            vs[r * stride + SWA_BLOCK:(r + 1) * stride, :] = v_ref[src, :].astype(BF16)

        bias = [(jnp.where(valid, -(sl * float(d)) * delta_f, NEG),
                 jnp.where(valid_first, -(sl * float(d)) * delta_f, NEG))
                for sl in (slope_a, slope_b)]

        def block(idx, carry, nb=nb, bias=bias):
            r = idx >> (nb.bit_length() - 1)
            n = idx & (nb - 1)
            qrow = pl.multiple_of(idx * SWA_BLOCK, SWA_BLOCK)
            krow = pl.multiple_of((idx + r) * SWA_BLOCK, SWA_BLOCK)
            qb = qs[pl.ds(qrow, SWA_BLOCK), :]
            kb = ks[pl.ds(krow, 2 * SWA_BLOCK), :]
            vb = vs[pl.ds(krow, 2 * SWA_BLOCK), :]
            outs, lses = [], []
            for hh in range(2):
                kh = jnp.where(head_a if hh == 0 else jnp.logical_not(head_a), kb, jnp.zeros_like(kb))
                s = _dot_nt(qb, kh)
                s = s + jnp.where(n == 0, bias[hh][1], bias[hh][0])
                m = jnp.max(s, axis=-1, keepdims=True)
                e = jnp.exp(s - m)
                l = jnp.sum(e, axis=-1, keepdims=True)
                outs.append(_dot(e.astype(BF16), vb) / l)
                lses.append(m + jnp.log(l))
            ores[pl.ds(qrow, SWA_BLOCK), :] = jnp.where(head_a, outs[0], outs[1])
            lres[pl.ds(qrow, SWA_BLOCK), :] = jnp.where(head_a, lses[0], lses[1])
            return carry

        lax.fori_loop(0, seq // SWA_BLOCK, block, 0)

        for r in range(d):
            dst = pl.ds(g * seq + r, sub, stride=d) if d > 1 else pl.ds(g * seq, sub)
            onat[dst, :] = ores[r * sub:(r + 1) * sub, :]
            lnat[dst, :] = lres[r * sub:(r + 1) * sub, :]

    l0, l1, l2 = (lnat[g * seq:(g + 1) * seq, :] for g in range(len(DILATIONS)))
    o0, o1, o2 = (onat[g * seq:(g + 1) * seq, :] for g in range(len(DILATIONS)))
    m = jnp.maximum(jnp.maximum(l0, l1), l2)
    w0, w1, w2 = jnp.exp(l0 - m), jnp.exp(l1 - m), jnp.exp(l2 - m)
    o_ref[...] = (w0 * o0 + w1 * o1 + w2 * o2) / (w0 + w1 + w2)


def _dilated(qkvb, batch, seq):
    npair = SWA_HEADS // 2
    spec = lambda off: pl.BlockSpec((seq, 2 * SWA_DIM), lambda b, p: (b, off + p))
    return pl.pallas_call(
        functools.partial(_dil_kernel, seq=seq),
        grid=(batch, npair),
        in_specs=[spec(0), spec(npair), spec(2 * npair)],
        out_specs=spec(0),
        out_shape=jax.ShapeDtypeStruct((batch * seq, SWA_W), F32),
        scratch_shapes=[
            pltpu.VMEM((seq, 2 * SWA_DIM), BF16),
            pltpu.VMEM((seq + max(DILATIONS) * SWA_BLOCK, 2 * SWA_DIM), BF16),
            pltpu.VMEM((seq + max(DILATIONS) * SWA_BLOCK, 2 * SWA_DIM), BF16),
            pltpu.VMEM((seq, 2 * SWA_DIM), F32),
            pltpu.VMEM((seq, 2 * SWA_DIM), F32),
            pltpu.VMEM((len(DILATIONS) * seq, 2 * SWA_DIM), F32),
            pltpu.VMEM((len(DILATIONS) * seq, 2 * SWA_DIM), F32),
        ],
        compiler_params=_params("parallel", "parallel"),
        name="dilated_attn",
    )(qkvb, qkvb, qkvb)


def _memkv_kernel(m_ref, g_ref, wk_ref, wv_ref, k_ref, v_ref):
    mn = _rms(m_ref[...], g_ref[...]).astype(BF16)
    k_ref[...] = _dot(mn, wk_ref[...]).astype(BF16)
    v_ref[...] = _dot(mn, wv_ref[...]).astype(BF16)


def _memkv(mem, g, wk, wv):
    t = mem.shape[0]
    tm = min(TM, t)
    row = pl.BlockSpec((tm, D_MODEL), lambda i: (i, 0))
    return pl.pallas_call(
        _memkv_kernel,
        grid=(t // tm,),
        in_specs=[row, _resident((1, D_MODEL)), _resident((D_MODEL, D_MODEL)),
                  _resident((D_MODEL, D_MODEL))],
        out_specs=[row, row],
        out_shape=[jax.ShapeDtypeStruct((t, D_MODEL), BF16)] * 2,
        compiler_params=_params("parallel"),
        name="mem_kv",
    )(mem, g, wk, wv)


def _outmem_kernel(x_ref, oa_ref, ob_ref, woa_ref, wob_ref, gmix_ref, gpre_ref, wq_ref,
                   k_ref, v_ref, wo_ref, gpost_ref, o_ref, att):
    mix = _dot(oa_ref[...].astype(BF16), woa_ref[...]) + _dot(ob_ref[...].astype(BF16), wob_ref[...])
    x = x_ref[...] + _rms(mix, gmix_ref[...])
    hq = _rms(x, gpre_ref[...]).astype(BF16)
    q = (_dot(hq, wq_ref[...]) * (MEM_DIM ** -0.5)).astype(BF16)
    for h in range(MEM_HEADS):
        hl = slice(h * MEM_DIM, (h + 1) * MEM_DIM)
        s = _dot_nt(q[:, hl], k_ref[:, hl])
        m = jnp.max(s, axis=-1, keepdims=True)
        e = jnp.exp(s - m)
        p = e / jnp.sum(e, axis=-1, keepdims=True)
        att[:, hl] = _dot(p.astype(BF16), v_ref[:, hl]).astype(BF16)
    c = _dot(att[...], wo_ref[...])
    o_ref[...] = x + _rms(c, gpost_ref[...])


def _outmem(x, oa, ob, woa, wob, gmix, gpre, wq, kmem, vmem, wo, gpost, seq, n_mem):
    t = x.shape[0]
    per_seq = seq // TM
    row = lambda w: pl.BlockSpec((TM, w), lambda i: (i, 0))
    kv = pl.BlockSpec((n_mem, D_MODEL), lambda i: (i // per_seq, 0))
    return pl.pallas_call(
        _outmem_kernel,
        grid=(t // TM,),
        in_specs=[row(D_MODEL), row(GDN_W), row(SWA_W), _resident((GDN_W, D_MODEL)),
                  _resident((SWA_W, D_MODEL)), _resident((1, D_MODEL)), _resident((1, D_MODEL)),
                  _resident((D_MODEL, D_MODEL)), kv, kv, _resident((D_MODEL, D_MODEL)),
                  _resident((1, D_MODEL))],
        out_specs=row(D_MODEL),
        out_shape=jax.ShapeDtypeStruct((t, D_MODEL), F32),
        scratch_shapes=[pltpu.VMEM((TM, D_MODEL), BF16)],
        compiler_params=_params("parallel"),
        name="out_mem_attn",
    )(x, oa, ob, woa, wob, gmix, gpre, wq, kmem, vmem, wo, gpost)


def _layer(x, mem, p, batch, seq, n_mem):
    bf = lambda w: w.astype(BF16)
    row = lambda g: g.reshape(1, -1)
    x = _ffn(x, row(p["ffn1_pre_g"]), bf(p["ffn1_w_gate"]), bf(p["ffn1_w_up"]),
             bf(p["ffn1_w_down"]), row(p["ffn1_post_g"]))

    w_in = p["w_in"]
    o_z, o_a, o_b = 3 * GDN_W, 4 * GDN_W, 4 * GDN_W + 2 * GDN_HEADS
    w_ab = w_in[:, o_a:o_b]
    wab = jnp.pad(w_ab, ((0, 0), (0, 128 - 2 * GDN_HEADS)))
    wabt = jnp.pad(w_ab.T, ((0, 16 - 2 * GDN_HEADS), (0, 0)))
    qkva, z, abc, abr, qkvb = _proj(x, row(p["mix_pre_g"]), bf(w_in[:, :o_z]), bf(w_in[:, o_z:o_a]),
                                    bf(wab), bf(wabt), bf(w_in[:, o_b:]))

    pad_lane = lambda v: jnp.pad(v.reshape(1, -1), ((0, 0), (0, 128 - GDN_HEADS)))
    pad_sub = lambda v: jnp.pad(v.reshape(-1, 1), ((0, 16 - GDN_HEADS), (0, 0)))
    cw = jnp.pad(p["gdn_conv_w"], ((0, 8 - GDN_CONV), (0, 0)))
    o_gdn = _gdn(qkva, z, abc, abr, cw, pad_lane(p["gdn_a_log"]), pad_lane(p["gdn_dt_bias"]),
                 pad_sub(p["gdn_a_log"]), pad_sub(p["gdn_dt_bias"]), row(p["gdn_norm_g"]), batch, seq)
    o_dil = _dilated(qkvb, batch, seq)

    kmem, vmem = _memkv(mem, row(p["mem_kv_g"]), bf(p["mem_wk"]), bf(p["mem_wv"]))
    w_out = bf(p["w_out"])
    x = _outmem(x, o_gdn, o_dil, w_out[:GDN_W], w_out[GDN_W:], row(p["mix_post_g"]),
                row(p["mem_pre_g"]), bf(p["mem_wq"]), kmem, vmem, bf(p["mem_wo"]),
                row(p["mem_post_g"]), seq, n_mem)

    return _ffn(x, row(p["ffn2_pre_g"]), bf(p["ffn2_w_gate"]), bf(p["ffn2_w_up"]),
                bf(p["ffn2_w_down"]), row(p["ffn2_post_g"]))


_NAMES = ("ffn1_pre_g", "ffn1_w_gate", "ffn1_w_up", "ffn1_w_down", "ffn1_post_g",
          "mix_pre_g", "w_in", "gdn_conv_w", "gdn_a_log", "gdn_dt_bias", "gdn_norm_g", "w_out",
          "mix_post_g", "mem_pre_g", "mem_kv_g", "mem_wq", "mem_wk", "mem_wv", "mem_wo",
          "mem_post_g", "ffn2_pre_g", "ffn2_w_gate", "ffn2_w_up", "ffn2_w_down", "ffn2_post_g")


def kernel(x, mem, ffn1_pre_g, ffn1_w_gate, ffn1_w_up, ffn1_w_down, ffn1_post_g, mix_pre_g, w_in, gdn_conv_w, gdn_a_log, gdn_dt_bias, gdn_norm_g, w_out, mix_post_g, mem_pre_g, mem_kv_g, mem_wq, mem_wk, mem_wv, mem_wo, mem_post_g, ffn2_pre_g, ffn2_w_gate, ffn2_w_up, ffn2_w_down, ffn2_post_g):
    stacked = dict(zip(_NAMES, (ffn1_pre_g, ffn1_w_gate, ffn1_w_up, ffn1_w_down, ffn1_post_g,
                                mix_pre_g, w_in, gdn_conv_w, gdn_a_log, gdn_dt_bias, gdn_norm_g, w_out,
                                mix_post_g, mem_pre_g, mem_kv_g, mem_wq, mem_wk, mem_wv, mem_wo,
                                mem_post_g, ffn2_pre_g, ffn2_w_gate, ffn2_w_up, ffn2_w_down,
                                ffn2_post_g)))
    batch, seq, _ = x.shape
    n_mem = mem.shape[1]
    xf = x.reshape(batch * seq, D_MODEL)
    memf = mem.reshape(batch * n_mem, D_MODEL)
    for l in range(ffn1_pre_g.shape[0]):
        xf = _layer(xf, memf, {k: v[l] for k, v in stacked.items()}, batch, seq, n_mem)
    return xf.reshape(batch, seq, D_MODEL)
```

```python
import functools

import jax
import jax.numpy as jnp
from jax import lax
from jax.experimental import pallas as pl
from jax.experimental.pallas import tpu as pltpu

F32 = jnp.float32
BF16 = jnp.bfloat16

D_MODEL = 1024
D_FF = 2816
EPS = 1e-6
GDN_HEADS = 4
GDN_DIM = 128
GDN_W = GDN_HEADS * GDN_DIM
GDN_CONV = 4
CHUNK = 64
SWA_HEADS = 8
SWA_DIM = 64
SWA_W = SWA_HEADS * SWA_DIM
SWA_BLOCK = 128
DILATIONS = (1, 4, 16)
MEM_HEADS = 4
MEM_DIM = D_MODEL // MEM_HEADS
NEG = -1e30

V7X_VMEM_BYTES = 64 * 1024 * 1024
VMEM_LIMIT = V7X_VMEM_BYTES - 8 * 1024 * 1024

TM = 512
FFN_FC = 256
GDN_GB = 4


def _rms(x, g):
    return x * lax.rsqrt(jnp.mean(x * x, axis=-1, keepdims=True) + EPS) * g


def _dot(a, b):
    return jnp.dot(a, b, preferred_element_type=F32)


def _dot_nt(a, b):
    return lax.dot_general(a, b, (((1,), (1,)), ((), ())), preferred_element_type=F32)


def _dot_tn(a, b):
    return lax.dot_general(a, b, (((0,), (0,)), ((), ())), preferred_element_type=F32)


def _resident(shape):
    return pl.BlockSpec(shape, lambda *_: (0,) * len(shape), pipeline_mode=pl.Buffered(1))


def _params(*sem):
    return pltpu.CompilerParams(dimension_semantics=sem, vmem_limit_bytes=VMEM_LIMIT)


def _ffn_kernel(x_ref, gpre_ref, wg_ref, wu_ref, wd_ref, gpost_ref, o_ref, a_ref):
    x = x_ref[...]
    xn = _rms(x, gpre_ref[...]).astype(BF16)
    for c in range(D_FF // FFN_FC):
        sl = slice(c * FFN_FC, (c + 1) * FFN_FC)
        g = _dot(xn, wg_ref[:, sl])
        u = _dot(xn, wu_ref[:, sl])
        a_ref[:, sl] = (g * jax.nn.sigmoid(g) * u).astype(BF16)
    f = _dot(a_ref[...], wd_ref[...])
    o_ref[...] = x + 0.5 * _rms(f, gpost_ref[...])


def _ffn(x, gpre, wg, wu, wd, gpost):
    t = x.shape[0]
    row = pl.BlockSpec((TM, D_MODEL), lambda i: (i, 0))
    return pl.pallas_call(
        _ffn_kernel,
        grid=(t // TM,),
        in_specs=[row, _resident((1, D_MODEL)), _resident((D_MODEL, D_FF)),
                  _resident((D_MODEL, D_FF)), _resident((D_FF, D_MODEL)),
                  _resident((1, D_MODEL))],
        out_specs=row,
        out_shape=jax.ShapeDtypeStruct((t, D_MODEL), F32),
        scratch_shapes=[pltpu.VMEM((TM, D_FF), BF16)],
        compiler_params=_params("parallel"),
        name="ffn",
    )(x, gpre, wg, wu, wd, gpost)


def _proj_kernel(x_ref, g_ref, wa_ref, wz_ref, wab_ref, wb_ref, qkva_ref, z_ref, abc_ref, qkvb_ref):
    h = _rms(x_ref[...], g_ref[...]).astype(BF16)
    qkva_ref[...] = _dot(h, wa_ref[...])
    z_ref[...] = _dot(h, wz_ref[...])
    abc_ref[...] = _dot(h, wab_ref[...])
    qkvb_ref[...] = _dot(h, wb_ref[...])


def _proj(x, g, wa, wz, wab, wb):
    t = x.shape[0]
    row = lambda w: pl.BlockSpec((TM, w), lambda i: (i, 0))
    return pl.pallas_call(
        _proj_kernel,
        grid=(t // TM,),
        in_specs=[row(D_MODEL), _resident((1, D_MODEL)), _resident(wa.shape),
                  _resident(wz.shape), _resident(wab.shape), _resident(wb.shape)],
        out_specs=[row(3 * GDN_W), row(GDN_W), row(128), row(3 * SWA_W)],
        out_shape=[jax.ShapeDtypeStruct((t, 3 * GDN_W), F32),
                   jax.ShapeDtypeStruct((t, GDN_W), F32),
                   jax.ShapeDtypeStruct((t, 128), F32),
                   jax.ShapeDtypeStruct((t, 3 * SWA_W), F32)],
        compiler_params=_params("parallel"),
        name="mixer_proj",
    )(x, g, wa, wz, wab, wb)


def _softplus(x):
    return jnp.maximum(x, 0.0) + jnp.log(1.0 + jnp.exp(-jnp.abs(x)))


def _level_masks():
    row = lax.broadcasted_iota(jnp.int32, (CHUNK, CHUNK), 0)
    col = lax.broadcasted_iota(jnp.int32, (CHUNK, CHUNK), 1)
    sels = []
    for lb in range(CHUNK.bit_length() - 1):
        sels.append(((row >> (lb + 1)) == (col >> (lb + 1))) & (((row >> lb) & 1) == 1)
                    & (((col >> lb) & 1) == 0))
    return row, col, sels


def _unit_lower_inverse(a_list, eye, sels):
    ts = [eye - jnp.where(sels[0], a, 0.0) for a in a_list]
    for sel in sels[1:]:
        tb = [t.astype(BF16) for t in ts]
        ps = [_dot(jnp.where(sel, a, 0.0).astype(BF16), t) for a, t in zip(a_list, tb)]
        ts = [t - _dot(t16, p.astype(BF16)) for t, t16, p in zip(ts, tb, ps)]
    return ts


def _gdn_kernel(x_ref, z_ref, abc_ref, cw_ref, alc_ref, dtc_ref, ng_ref, o_ref, xpad, st):
    nbatch = x_ref.shape[0]

    @pl.when(pl.program_id(0) == 0)
    def _():
        xpad[:, 0:8, :] = jnp.zeros((nbatch, 8, 3 * GDN_W), F32)
        st[...] = jnp.zeros_like(st)

    row, col, sels = _level_masks()
    tril = col <= row
    strict = col < row
    eye = jnp.where(row == col, 1.0, 0.0).astype(F32)
    lower = jnp.where(tril, 1.0, 0.0).astype(F32)
    upper = jnp.where(row <= col, 1.0, 0.0).astype(F32)
    hi = lax.Precision.HIGHEST

    def group(gi, carry):
        items = []
        for bb in range(GDN_GB):
            b = gi * GDN_GB + bb
            xpad[b, 8:8 + CHUNK, :] = x_ref[b]
            y = jnp.zeros((CHUNK, 3 * GDN_W), F32)
            for j in range(GDN_CONV):
                y = y + cw_ref[j:j + 1, :] * xpad[b, 5 + j:5 + j + CHUNK, :]
            xpad[b, 0:8, :] = xpad[b, CHUNK:CHUNK + 8, :]
            y = y * jax.nn.sigmoid(y)
            abc = abc_ref[b]
            g_col = -jnp.exp(alc_ref[...]) * _softplus(abc + dtc_ref[...])
            gc_col = jnp.dot(lower, g_col, precision=hi, preferred_element_type=F32)
            gc_row = lax.dot_general(g_col, upper, (((0,), (0,)), ((), ())), precision=hi,
                                     preferred_element_type=F32)
            beta_all = jax.nn.sigmoid(abc)
            for h in range(GDN_HEADS):
                q = y[:, h * GDN_DIM:(h + 1) * GDN_DIM]
                k = y[:, GDN_W + h * GDN_DIM:GDN_W + (h + 1) * GDN_DIM]
                v = y[:, 2 * GDN_W + h * GDN_DIM:2 * GDN_W + (h + 1) * GDN_DIM]
                q = q * lax.rsqrt(jnp.sum(q * q, axis=-1, keepdims=True) + EPS) * (GDN_DIM ** -0.5)
                k = k * lax.rsqrt(jnp.sum(k * k, axis=-1, keepdims=True) + EPS)
                items.append(dict(b=b, h=h, q=q, k=k, v=v, gcc=gc_col[:, h:h + 1],
                                  gcr=gc_row[h:h + 1, :], gl=gc_col[CHUNK - 1:CHUNK, h:h + 1],
                                  beta=beta_all[:, 4 + h:5 + h]))

        kqs = []
        for it in items:
            it["kb"] = it["k"] * it["beta"]
            kqs.append(_dot_nt(jnp.concatenate([it["kb"], it["q"]], axis=0).astype(BF16),
                               it["k"].astype(BF16)))
        a_list = []
        for it, kq in zip(items, kqs):
            decay = jnp.exp(jnp.where(tril, it["gcc"] - it["gcr"], NEG))
            a_list.append(jnp.where(strict, kq[:CHUNK] * decay, 0.0))
            it["qk"] = jnp.where(tril, kq[CHUNK:] * decay, 0.0).astype(BF16)
        ts = _unit_lower_inverse(a_list, eye, sels)
        uws = []
        for it, t in zip(items, ts):
            eg = jnp.exp(it["gcc"])
            rhs = jnp.concatenate([it["v"] * it["beta"], it["kb"] * eg], axis=1).astype(BF16)
            uws.append(_dot(t.astype(BF16), rhs))
            it["qd"] = (it["q"] * eg).astype(BF16)
            it["kd"] = (it["k"] * jnp.exp(it["gl"] - it["gcc"])).astype(BF16)

        states, wss = [], []
        for it, uw in zip(items, uws):
            s = st[it["b"] * GDN_HEADS + it["h"]]
            states.append(s)
            lhs = jnp.concatenate([uw[:, GDN_DIM:].astype(BF16), it["qd"]], axis=0)
            wss.append(_dot(lhs, s.astype(BF16)))
        for it, uw, s, ws in zip(items, uws, states, wss):
            vb = (uw[:, :GDN_DIM] - ws[:CHUNK]).astype(BF16)
            o = ws[CHUNK:] + _dot(it["qk"], vb)
            st[it["b"] * GDN_HEADS + it["h"]] = s * jnp.exp(it["gl"]) + _dot_tn(it["kd"], vb)
            hl = slice(it["h"] * GDN_DIM, (it["h"] + 1) * GDN_DIM)
            zz = z_ref[it["b"], :, hl]
            o = o * lax.rsqrt(jnp.mean(o * o, axis=-1, keepdims=True) + EPS)
            o_ref[it["b"], :, hl] = o * ng_ref[...] * (zz * jax.nn.sigmoid(zz))
        return carry

    lax.fori_loop(0, nbatch // GDN_GB, group, 0)


def _gdn(qkva, z, abc, cw, alc, dtc, ng):
    batch, seq, _ = qkva.shape
    blk = lambda w: pl.BlockSpec((batch, CHUNK, w), lambda c: (0, c, 0))
    const = lambda shape: pl.BlockSpec(shape, lambda c: (0, 0))
    return pl.pallas_call(
        _gdn_kernel,
        grid=(seq // CHUNK,),
        in_specs=[blk(3 * GDN_W), blk(GDN_W), blk(128), const((8, 3 * GDN_W)),
                  const((1, 128)), const((1, 128)), const((1, GDN_DIM))],
        out_specs=blk(GDN_W),
        out_shape=jax.ShapeDtypeStruct((batch, seq, GDN_W), F32),
        scratch_shapes=[
            pltpu.VMEM((batch, CHUNK + 8, 3 * GDN_W), F32),
            pltpu.VMEM((batch * GDN_HEADS, GDN_DIM, GDN_DIM), F32),
        ],
        compiler_params=_params("arbitrary"),
        name="gdn",
    )(qkva, z, abc, cw, alc, dtc, ng)


def _dil_kernel(q_ref, k_ref, v_ref, o_ref, qs, ks, vs, ores, lres, onat, lnat, *, seq):
    hp = pl.program_id(1)
    lane = lax.broadcasted_iota(jnp.int32, (1, 2 * SWA_DIM), 1)
    head_a = lane < SWA_DIM
    row = lax.broadcasted_iota(jnp.int32, (SWA_BLOCK, 2 * SWA_BLOCK), 0)
    col = lax.broadcasted_iota(jnp.int32, (SWA_BLOCK, 2 * SWA_BLOCK), 1)
    delta = row + SWA_BLOCK - col
    valid = (delta >= 0) & (delta <= SWA_BLOCK)
    valid_first = valid & (col >= SWA_BLOCK)
    delta_f = delta.astype(F32)

    def slope(head):
        e = jnp.full((SWA_BLOCK, 2 * SWA_BLOCK), 126, jnp.int32) - head
        return lax.bitcast_convert_type(lax.shift_left(e, jnp.full_like(e, 23)), F32)

    slope_a = slope(2 * hp)
    slope_b = slope(2 * hp + 1)

    for g, d in enumerate(DILATIONS):
        sub = seq // d
        nb = sub // SWA_BLOCK
        stride = sub + SWA_BLOCK
        for r in range(d):
            src = pl.ds(r, sub, stride=d) if d > 1 else pl.ds(0, sub)
            qs[r * sub:(r + 1) * sub, :] = (q_ref[src, :] * (SWA_DIM ** -0.5)).astype(BF16)
            ks[r * stride:r * stride + SWA_BLOCK, :] = jnp.zeros((SWA_BLOCK, 2 * SWA_DIM), BF16)
            vs[r * stride:r * stride + SWA_BLOCK, :] = jnp.zeros((SWA_BLOCK, 2 * SWA_DIM), BF16)
            ks[r * stride + SWA_BLOCK:(r + 1) * stride, :] = k_ref[src, :].astype(BF16)
            vs[r * stride + SWA_BLOCK:(r + 1) * stride, :] = v_ref[src, :].astype(BF16)

        bias = [(jnp.where(valid, -(sl * float(d)) * delta_f, NEG),
                 jnp.where(valid_first, -(sl * float(d)) * delta_f, NEG))
                for sl in (slope_a, slope_b)]

        def block(idx, carry, nb=nb, bias=bias):
            r = idx >> (nb.bit_length() - 1)
            n = idx & (nb - 1)
            qrow = pl.multiple_of(idx * SWA_BLOCK, SWA_BLOCK)
            krow = pl.multiple_of((idx + r) * SWA_BLOCK, SWA_BLOCK)
            qb = qs[pl.ds(qrow, SWA_BLOCK), :]
            kb = ks[pl.ds(krow, 2 * SWA_BLOCK), :]
            vb = vs[pl.ds(krow, 2 * SWA_BLOCK), :]
            outs, lses = [], []
            for hh in range(2):
                kh = jnp.where(head_a if hh == 0 else jnp.logical_not(head_a), kb, jnp.zeros_like(kb))
                s = _dot_nt(qb, kh)
                s = s + jnp.where(n == 0, bias[hh][1], bias[hh][0])
                m = jnp.max(s, axis=-1, keepdims=True)
                e = jnp.exp(s - m)
                l = jnp.sum(e, axis=-1, keepdims=True)
                outs.append(_dot(e.astype(BF16), vb) / l)
                lses.append(m + jnp.log(l))
            ores[pl.ds(qrow, SWA_BLOCK), :] = jnp.where(head_a, outs[0], outs[1])
            lres[pl.ds(qrow, SWA_BLOCK), :] = jnp.where(head_a, lses[0], lses[1])
            return carry

        lax.fori_loop(0, seq // SWA_BLOCK, block, 0)

        for r in range(d):
            dst = pl.ds(g * seq + r, sub, stride=d) if d > 1 else pl.ds(g * seq, sub)
            onat[dst, :] = ores[r * sub:(r + 1) * sub, :]
            lnat[dst, :] = lres[r * sub:(r + 1) * sub, :]

    l0, l1, l2 = (lnat[g * seq:(g + 1) * seq, :] for g in range(len(DILATIONS)))
    o0, o1, o2 = (onat[g * seq:(g + 1) * seq, :] for g in range(len(DILATIONS)))
    m = jnp.maximum(jnp.maximum(l0, l1), l2)
    w0, w1, w2 = jnp.exp(l0 - m), jnp.exp(l1 - m), jnp.exp(l2 - m)
    o_ref[...] = (w0 * o0 + w1 * o1 + w2 * o2) / (w0 + w1 + w2)


def _dilated(qkvb, batch, seq):
    npair = SWA_HEADS // 2
    spec = lambda off: pl.BlockSpec((seq, 2 * SWA_DIM), lambda b, p: (b, off + p))
    return pl.pallas_call(
        functools.partial(_dil_kernel, seq=seq),
        grid=(batch, npair),
        in_specs=[spec(0), spec(npair), spec(2 * npair)],
        out_specs=spec(0),
        out_shape=jax.ShapeDtypeStruct((batch * seq, SWA_W), F32),
        scratch_shapes=[
            pltpu.VMEM((seq, 2 * SWA_DIM), BF16),
            pltpu.VMEM((seq + max(DILATIONS) * SWA_BLOCK, 2 * SWA_DIM), BF16),
            pltpu.VMEM((seq + max(DILATIONS) * SWA_BLOCK, 2 * SWA_DIM), BF16),
            pltpu.VMEM((seq, 2 * SWA_DIM), F32),
            pltpu.VMEM((seq, 2 * SWA_DIM), F32),
            pltpu.VMEM((len(DILATIONS) * seq, 2 * SWA_DIM), F32),
            pltpu.VMEM((len(DILATIONS) * seq, 2 * SWA_DIM), F32),
        ],
        compiler_params=_params("parallel", "parallel"),
        name="dilated_attn",
    )(qkvb, qkvb, qkvb)


def _memkv_kernel(m_ref, g_ref, wk_ref, wv_ref, k_ref, v_ref):
    mn = _rms(m_ref[...], g_ref[...]).astype(BF16)
    k_ref[...] = _dot(mn, wk_ref[...]).astype(BF16)
    v_ref[...] = _dot(mn, wv_ref[...]).astype(BF16)


def _memkv(mem, g, wk, wv):
    t = mem.shape[0]
    tm = min(TM, t)
    row = pl.BlockSpec((tm, D_MODEL), lambda i: (i, 0))
    return pl.pallas_call(
        _memkv_kernel,
        grid=(t // tm,),
        in_specs=[row, _resident((1, D_MODEL)), _resident((D_MODEL, D_MODEL)),
                  _resident((D_MODEL, D_MODEL))],
        out_specs=[row, row],
        out_shape=[jax.ShapeDtypeStruct((t, D_MODEL), BF16)] * 2,
        compiler_params=_params("parallel"),
        name="mem_kv",
    )(mem, g, wk, wv)


def _outmem_kernel(x_ref, oa_ref, ob_ref, woa_ref, wob_ref, gmix_ref, gpre_ref, wq_ref,
                   k_ref, v_ref, wo_ref, gpost_ref, o_ref, att):
    mix = _dot(oa_ref[...].astype(BF16), woa_ref[...]) + _dot(ob_ref[...].astype(BF16), wob_ref[...])
    x = x_ref[...] + _rms(mix, gmix_ref[...])
    hq = _rms(x, gpre_ref[...]).astype(BF16)
    q = (_dot(hq, wq_ref[...]) * (MEM_DIM ** -0.5)).astype(BF16)
    for h in range(MEM_HEADS):
        hl = slice(h * MEM_DIM, (h + 1) * MEM_DIM)
        s = _dot_nt(q[:, hl], k_ref[:, hl])
        m = jnp.max(s, axis=-1, keepdims=True)
        e = jnp.exp(s - m)
        p = e / jnp.sum(e, axis=-1, keepdims=True)
        att[:, hl] = _dot(p.astype(BF16), v_ref[:, hl]).astype(BF16)
    c = _dot(att[...], wo_ref[...])
    o_ref[...] = x + _rms(c, gpost_ref[...])


def _outmem(x, oa, ob, woa, wob, gmix, gpre, wq, kmem, vmem, wo, gpost, seq, n_mem):
    t = x.shape[0]
    per_seq = seq // TM
    row = lambda w: pl.BlockSpec((TM, w), lambda i: (i, 0))
    kv = pl.BlockSpec((n_mem, D_MODEL), lambda i: (i // per_seq, 0))
    return pl.pallas_call(
        _outmem_kernel,
        grid=(t // TM,),
        in_specs=[row(D_MODEL), row(GDN_W), row(SWA_W), _resident((GDN_W, D_MODEL)),
                  _resident((SWA_W, D_MODEL)), _resident((1, D_MODEL)), _resident((1, D_MODEL)),
                  _resident((D_MODEL, D_MODEL)), kv, kv, _resident((D_MODEL, D_MODEL)),
                  _resident((1, D_MODEL))],
        out_specs=row(D_MODEL),
        out_shape=jax.ShapeDtypeStruct((t, D_MODEL), F32),
        scratch_shapes=[pltpu.VMEM((TM, D_MODEL), BF16)],
        compiler_params=_params("parallel"),
        name="out_mem_attn",
    )(x, oa, ob, woa, wob, gmix, gpre, wq, kmem, vmem, wo, gpost)


def _layer(x, mem, p, batch, seq, n_mem):
    bf = lambda w: w.astype(BF16)
    row = lambda g: g.reshape(1, -1)
    x = _ffn(x, row(p["ffn1_pre_g"]), bf(p["ffn1_w_gate"]), bf(p["ffn1_w_up"]),
             bf(p["ffn1_w_down"]), row(p["ffn1_post_g"]))

    w_in = p["w_in"]
    o_z, o_a, o_b = 3 * GDN_W, 4 * GDN_W, 4 * GDN_W + 2 * GDN_HEADS
    w_ab = w_in[:, o_a:o_b]
    wab = jnp.pad(w_ab, ((0, 0), (0, 128 - 2 * GDN_HEADS)))
    qkva, z, abc, qkvb = _proj(x, row(p["mix_pre_g"]), bf(w_in[:, :o_z]), bf(w_in[:, o_z:o_a]),
                               bf(wab), bf(w_in[:, o_b:]))

    pad_lane = lambda v: jnp.pad(v.reshape(1, -1), ((0, 0), (0, 128 - GDN_HEADS)))
    cw = jnp.pad(p["gdn_conv_w"], ((0, 8 - GDN_CONV), (0, 0)))
    per_seq = lambda a: a.reshape(batch, seq, a.shape[-1])
    o_gdn = _gdn(per_seq(qkva), per_seq(z), per_seq(abc), cw, pad_lane(p["gdn_a_log"]),
                 pad_lane(p["gdn_dt_bias"]), row(p["gdn_norm_g"])).reshape(batch * seq, GDN_W)
    o_dil = _dilated(qkvb, batch, seq)

    kmem, vmem = _memkv(mem, row(p["mem_kv_g"]), bf(p["mem_wk"]), bf(p["mem_wv"]))
    w_out = bf(p["w_out"])
    x = _outmem(x, o_gdn, o_dil, w_out[:GDN_W], w_out[GDN_W:], row(p["mix_post_g"]),
                row(p["mem_pre_g"]), bf(p["mem_wq"]), kmem, vmem, bf(p["mem_wo"]),
                row(p["mem_post_g"]), seq, n_mem)

    return _ffn(x, row(p["ffn2_pre_g"]), bf(p["ffn2_w_gate"]), bf(p["ffn2_w_up"]),
                bf(p["ffn2_w_down"]), row(p["ffn2_post_g"]))


_NAMES = ("ffn1_pre_g", "ffn1_w_gate", "ffn1_w_up", "ffn1_w_down", "ffn1_post_g",
          "mix_pre_g", "w_in", "gdn_conv_w", "gdn_a_log", "gdn_dt_bias", "gdn_norm_g", "w_out",
          "mix_post_g", "mem_pre_g", "mem_kv_g", "mem_wq", "mem_wk", "mem_wv", "mem_wo",
          "mem_post_g", "ffn2_pre_g", "ffn2_w_gate", "ffn2_w_up", "ffn2_w_down", "ffn2_post_g")


def kernel(x, mem, ffn1_pre_g, ffn1_w_gate, ffn1_w_up, ffn1_w_down, ffn1_post_g, mix_pre_g, w_in, gdn_conv_w, gdn_a_log, gdn_dt_bias, gdn_norm_g, w_out, mix_post_g, mem_pre_g, mem_kv_g, mem_wq, mem_wk, mem_wv, mem_wo, mem_post_g, ffn2_pre_g, ffn2_w_gate, ffn2_w_up, ffn2_w_down, ffn2_post_g):
    stacked = dict(zip(_NAMES, (ffn1_pre_g, ffn1_w_gate, ffn1_w_up, ffn1_w_down, ffn1_post_g,
                                mix_pre_g, w_in, gdn_conv_w, gdn_a_log, gdn_dt_bias, gdn_norm_g, w_out,
                                mix_post_g, mem_pre_g, mem_kv_g, mem_wq, mem_wk, mem_wv, mem_wo,
                                mem_post_g, ffn2_pre_g, ffn2_w_gate, ffn2_w_up, ffn2_w_down,
                                ffn2_post_g)))
    batch, seq, _ = x.shape
    n_mem = mem.shape[1]
    xf = x.reshape(batch * seq, D_MODEL)
    memf = mem.reshape(batch * n_mem, D_MODEL)
    for l in range(ffn1_pre_g.shape[0]):
        xf = _layer(xf, memf, {k: v[l] for k, v in stacked.items()}, batch, seq, n_mem)
    return xf.reshape(batch, seq, D_MODEL)
```

```python
import jax
import jax.numpy as jnp
from jax import lax
from jax.experimental import pallas as pl
from jax.experimental.pallas import tpu as pltpu

F32 = jnp.float32
BF16 = jnp.bfloat16

D_MODEL = 1024
D_FF = 2816
EPS = 1e-6
GDN_HEADS = 4
GDN_DIM = 128
GDN_W = GDN_HEADS * GDN_DIM
GDN_CONV = 4
CHUNK = 64
SWA_HEADS = 8
SWA_DIM = 64
SWA_W = SWA_HEADS * SWA_DIM
SWA_BLOCK = 128
DILATIONS = (1, 4, 16)
RES = 16
MEM_HEADS = 4
MEM_DIM = D_MODEL // MEM_HEADS
NEG = -1e30

V7X_VMEM_BYTES = 64 * 1024 * 1024
VMEM_LIMIT = V7X_VMEM_BYTES - 8 * 1024 * 1024

TM = 512
HALO = 16
FFN_FC = 256
GDN_GB = 4
DIL_UNROLL = 4


def _rms(x, g):
    return x * lax.rsqrt(jnp.mean(x * x, axis=-1, keepdims=True) + EPS) * g


def _dot(a, b):
    return jnp.dot(a, b, preferred_element_type=F32)


def _dot_nt(a, b):
    return lax.dot_general(a, b, (((1,), (1,)), ((), ())), preferred_element_type=F32)


def _dot_tn(a, b):
    return lax.dot_general(a, b, (((0,), (0,)), ((), ())), preferred_element_type=F32)


def _resident(shape):
    return pl.BlockSpec(shape, lambda *_: (0,) * len(shape), pipeline_mode=pl.Buffered(1))


def _params(*sem):
    return pltpu.CompilerParams(dimension_semantics=sem, vmem_limit_bytes=VMEM_LIMIT)


def _ffn_kernel(x_ref, gpre_ref, wg_ref, wu_ref, wd_ref, gpost_ref, o_ref, a_ref):
    x = x_ref[...]
    xn = _rms(x, gpre_ref[...]).astype(BF16)
    for c in range(D_FF // FFN_FC):
        sl = slice(c * FFN_FC, (c + 1) * FFN_FC)
        g = _dot(xn, wg_ref[:, sl])
        u = _dot(xn, wu_ref[:, sl])
        a_ref[:, sl] = (g * jax.nn.sigmoid(g) * u).astype(BF16)
    f = _dot(a_ref[...], wd_ref[...])
    o_ref[...] = x + 0.5 * _rms(f, gpost_ref[...])


def _ffn(x, gpre, wg, wu, wd, gpost):
    t = x.shape[0]
    row = pl.BlockSpec((TM, D_MODEL), lambda i: (i, 0))
    return pl.pallas_call(
        _ffn_kernel,
        grid=(t // TM,),
        in_specs=[row, _resident((1, D_MODEL)), _resident((D_MODEL, D_FF)),
                  _resident((D_MODEL, D_FF)), _resident((D_FF, D_MODEL)),
                  _resident((1, D_MODEL))],
        out_specs=row,
        out_shape=jax.ShapeDtypeStruct((t, D_MODEL), F32),
        scratch_shapes=[pltpu.VMEM((TM, D_FF), BF16)],
        compiler_params=_params("parallel"),
        name="ffn",
    )(x, gpre, wg, wu, wd, gpost)


def _proj_kernel(x_ref, xh_ref, g_ref, wa_ref, wz_ref, wab_ref, wb_ref, cw_ref,
                 qkva_ref, z_ref, abc_ref, qkvb_ref, ra, rb, *, per_seq):
    g = g_ref[...]
    h = _rms(x_ref[...], g).astype(BF16)
    hh = _rms(xh_ref[...], g).astype(BF16)
    ra[...] = _dot(jnp.concatenate([hh, h], axis=0), wa_ref[...])

    @pl.when(pl.program_id(0) % per_seq == 0)
    def _():
        ra[0:HALO, :] = jnp.zeros((HALO, 3 * GDN_W), F32)

    z_ref[...] = _dot(h, wz_ref[...])
    abc_ref[...] = _dot(h, wab_ref[...])
    yb = _dot(h, wb_ref[...])
    for l in range(3 * SWA_W // 128):
        lanes = slice(128 * l, 128 * l + 128)
        rb[l] = yb[:, lanes]
        for r in range(RES):
            qkvb_ref[0, r, :, lanes] = rb[l, pl.ds(r, TM // RES, stride=RES), :]

    for c in range(TM // CHUNK):
        base = HALO + c * CHUNK - (GDN_CONV - 1)
        y = jnp.zeros((CHUNK, 3 * GDN_W), F32)
        for j in range(GDN_CONV):
            y = y + cw_ref[j:j + 1, :] * ra[base + j:base + j + CHUNK, :]
        y = y * jax.nn.sigmoid(y)
        rows = slice(c * CHUNK, (c + 1) * CHUNK)
        for hd in range(GDN_HEADS):
            ql = slice(hd * GDN_DIM, (hd + 1) * GDN_DIM)
            kl = slice(GDN_W + hd * GDN_DIM, GDN_W + (hd + 1) * GDN_DIM)
            q = y[:, ql]
            k = y[:, kl]
            qkva_ref[rows, ql] = q * lax.rsqrt(jnp.sum(q * q, axis=-1, keepdims=True) + EPS) * (GDN_DIM ** -0.5)
            qkva_ref[rows, kl] = k * lax.rsqrt(jnp.sum(k * k, axis=-1, keepdims=True) + EPS)
        qkva_ref[rows, 2 * GDN_W:] = y[:, 2 * GDN_W:]


def _proj(x, g, wa, wz, wab, wb, cw, batch, seq):
    t = x.shape[0]
    per_seq = seq // TM
    row = lambda w: pl.BlockSpec((TM, w), lambda i: (i, 0))
    halo = pl.BlockSpec((HALO, D_MODEL), lambda i: (jnp.maximum(i * (TM // HALO) - 1, 0), 0))
    return pl.pallas_call(
        lambda *refs: _proj_kernel(*refs, per_seq=per_seq),
        grid=(t // TM,),
        in_specs=[row(D_MODEL), halo, _resident((1, D_MODEL)), _resident(wa.shape),
                  _resident(wz.shape), _resident(wab.shape), _resident(wb.shape),
                  _resident(cw.shape)],
        out_specs=[row(3 * GDN_W), row(GDN_W), row(128),
                   pl.BlockSpec((1, RES, TM // RES, 3 * SWA_W),
                                lambda i: (i // per_seq, 0, i % per_seq, 0))],
        out_shape=[jax.ShapeDtypeStruct((t, 3 * GDN_W), F32),
                   jax.ShapeDtypeStruct((t, GDN_W), F32),
                   jax.ShapeDtypeStruct((t, 128), F32),
                   jax.ShapeDtypeStruct((batch, RES, seq // RES, 3 * SWA_W), F32)],
        scratch_shapes=[pltpu.VMEM((HALO + TM, 3 * GDN_W), F32),
                        pltpu.VMEM((3 * SWA_W // 128, TM, 128), F32)],
        compiler_params=_params("parallel"),
        name="mixer_proj",
    )(x, x, g, wa, wz, wab, wb, cw)


def _softplus(x):
    return jnp.maximum(x, 0.0) + jnp.log(1.0 + jnp.exp(-jnp.abs(x)))


def _level_masks():
    row = lax.broadcasted_iota(jnp.int32, (CHUNK, CHUNK), 0)
    col = lax.broadcasted_iota(jnp.int32, (CHUNK, CHUNK), 1)
    sels = []
    for lb in range(CHUNK.bit_length() - 1):
        sels.append(((row >> (lb + 1)) == (col >> (lb + 1))) & (((row >> lb) & 1) == 1)
                    & (((col >> lb) & 1) == 0))
    return row, col, sels


def _unit_lower_inverse(a_list, eye, sels):
    ts = [eye - jnp.where(sels[0], a, 0.0) for a in a_list]
    for sel in sels[1:]:
        tb = [t.astype(BF16) for t in ts]
        ps = [_dot(jnp.where(sel, a, 0.0).astype(BF16), t) for a, t in zip(a_list, tb)]
        ts = [t - _dot(t16, p.astype(BF16)) for t, t16, p in zip(ts, tb, ps)]
    return ts


def _gdn_kernel(x_ref, z_ref, abc_ref, alc_ref, dtc_ref, ng_ref, o_ref, st):
    nbatch = x_ref.shape[0]

    @pl.when(pl.program_id(0) == 0)
    def _():
        st[...] = jnp.zeros_like(st)

    row, col, sels = _level_masks()
    tril = col <= row
    strict = col < row
    eye = jnp.where(row == col, 1.0, 0.0).astype(F32)
    lower = jnp.where(tril, 1.0, 0.0).astype(F32)
    upper = jnp.where(row <= col, 1.0, 0.0).astype(F32)
    hi = lax.Precision.HIGHEST

    def group(gi, carry):
        items = []
        for bb in range(GDN_GB):
            b = gi * GDN_GB + bb
            abc = abc_ref[b]
            g_col = -jnp.exp(alc_ref[...]) * _softplus(abc + dtc_ref[...])
            gc_col = jnp.dot(lower, g_col, precision=hi, preferred_element_type=F32)
            gc_row = lax.dot_general(g_col, upper, (((0,), (0,)), ((), ())), precision=hi,
                                     preferred_element_type=F32)
            beta_all = jax.nn.sigmoid(abc)
            for h in range(GDN_HEADS):
                items.append(dict(
                    b=b, h=h,
                    q=x_ref[b, :, h * GDN_DIM:(h + 1) * GDN_DIM],
                    k=x_ref[b, :, GDN_W + h * GDN_DIM:GDN_W + (h + 1) * GDN_DIM],
                    v=x_ref[b, :, 2 * GDN_W + h * GDN_DIM:2 * GDN_W + (h + 1) * GDN_DIM],
                    gcc=gc_col[:, h:h + 1], gcr=gc_row[h:h + 1, :],
                    gl=gc_col[CHUNK - 1:CHUNK, h:h + 1], beta=beta_all[:, 4 + h:5 + h]))

        kqs = []
        for it in items:
            it["kb"] = it["k"] * it["beta"]
            kqs.append(_dot_nt(jnp.concatenate([it["kb"], it["q"]], axis=0).astype(BF16),
                               it["k"].astype(BF16)))
        a_list = []
        for it, kq in zip(items, kqs):
            decay = jnp.exp(jnp.where(tril, it["gcc"] - it["gcr"], NEG))
            a_list.append(jnp.where(strict, kq[:CHUNK] * decay, 0.0))
            it["qk"] = jnp.where(tril, kq[CHUNK:] * decay, 0.0).astype(BF16)
        ts = _unit_lower_inverse(a_list, eye, sels)
        uws = []
        for it, t in zip(items, ts):
            eg = jnp.exp(it["gcc"])
            rhs = jnp.concatenate([it["v"] * it["beta"], it["kb"] * eg], axis=1).astype(BF16)
            uws.append(_dot(t.astype(BF16), rhs))
            it["qd"] = (it["q"] * eg).astype(BF16)
            it["kd"] = (it["k"] * jnp.exp(it["gl"] - it["gcc"])).astype(BF16)

        states, wss = [], []
        for it, uw in zip(items, uws):
            s = st[it["b"] * GDN_HEADS + it["h"]]
            states.append(s)
            lhs = jnp.concatenate([uw[:, GDN_DIM:].astype(BF16), it["qd"]], axis=0)
            wss.append(_dot(lhs, s.astype(BF16)))
        for it, uw, s, ws in zip(items, uws, states, wss):
            vb = (uw[:, :GDN_DIM] - ws[:CHUNK]).astype(BF16)
            o = ws[CHUNK:] + _dot(it["qk"], vb)
            st[it["b"] * GDN_HEADS + it["h"]] = s * jnp.exp(it["gl"]) + _dot_tn(it["kd"], vb)
            hl = slice(it["h"] * GDN_DIM, (it["h"] + 1) * GDN_DIM)
            zz = z_ref[it["b"], :, hl]
            o = o * lax.rsqrt(jnp.mean(o * o, axis=-1, keepdims=True) + EPS)
            o_ref[it["b"], :, hl] = o * ng_ref[...] * (zz * jax.nn.sigmoid(zz))
        return carry

    lax.fori_loop(0, nbatch // GDN_GB, group, 0)


def _gdn(qkva, z, abc, alc, dtc, ng):
    batch, seq, _ = qkva.shape
    blk = lambda w: pl.BlockSpec((batch, CHUNK, w), lambda c: (0, c, 0))
    const = lambda shape: pl.BlockSpec(shape, lambda c: (0, 0))
    return pl.pallas_call(
        _gdn_kernel,
        grid=(seq // CHUNK,),
        in_specs=[blk(3 * GDN_W), blk(GDN_W), blk(128),
                  const((1, 128)), const((1, 128)), const((1, GDN_DIM))],
        out_specs=blk(GDN_W),
        out_shape=jax.ShapeDtypeStruct((batch, seq, GDN_W), F32),
        scratch_shapes=[pltpu.VMEM((batch * GDN_HEADS, GDN_DIM, GDN_DIM), F32)],
        compiler_params=_params("arbitrary"),
        name="gdn",
    )(qkva, z, abc, alc, dtc, ng)


def _dil_bias(delta, dilation, slopes):
    valid = (delta >= 0) & (delta <= SWA_BLOCK)
    df = delta.astype(F32)
    return jnp.concatenate([jnp.where(valid, -(s * float(dilation)) * df, NEG) for s in slopes], axis=0)


def _dil_tables(hp, b1, b4, b16):
    def iota(shape, dim):
        return lax.broadcasted_iota(jnp.int32, shape, dim)

    def slope(head, shape):
        e = jnp.full(shape, 126, jnp.int32) - head
        return lax.bitcast_convert_type(lax.shift_left(e, jnp.full_like(e, 23)), F32)

    wide = (SWA_BLOCK, 2 * SWA_BLOCK)
    sl = [slope(2 * hp, wide), slope(2 * hp + 1, wide)]
    p, c = iota(wide, 0), iota(wide, 1)
    d = 16 * ((p & 7) - (c & 15)) + ((p >> 3) - (c >> 4))
    b1[0] = _dil_bias(d + 16 * 8, 1, sl)
    b1[1] = _dil_bias(d, 1, sl)
    d = 4 * ((p & 31) - (c & 63)) + ((p >> 5) - (c >> 6))
    b4[0] = _dil_bias(d + 4 * 32, 4, sl)
    b4[1] = _dil_bias(d, 4, sl)
    sq = (SWA_BLOCK, SWA_BLOCK)
    b16[...] = _dil_bias(iota(sq, 0) - iota(sq, 1), 16, [slope(2 * hp, sq), slope(2 * hp + 1, sq)])


def _attend(tiles, head_a):
    scores = []
    for q, k, v, bias in tiles:
        qa = jnp.where(head_a, q, 0.0)
        qs = (jnp.concatenate([qa, q - qa], axis=0) * (SWA_DIM ** -0.5)).astype(BF16)
        scores.append(_dot_nt(qs, k.astype(BF16)) + bias)
    probs = []
    for s in scores:
        m = jnp.max(s, axis=-1, keepdims=True)
        e = jnp.exp(s - m)
        l = jnp.sum(e, axis=-1, keepdims=True)
        probs.append((e.astype(BF16), 1.0 / l, m + jnp.log(l)))
    outs = []
    for (q, k, v, bias), (e, inv, lse) in zip(tiles, probs):
        o = _dot(e, v.astype(BF16)) * inv
        outs.append((jnp.where(head_a, o[:SWA_BLOCK], o[SWA_BLOCK:]),
                     jnp.where(head_a, lse[:SWA_BLOCK], lse[SWA_BLOCK:])))
    return outs


def _dil_kernel(q_ref, k_ref, v_ref, o_ref, b1, b4, b16, o1, l1, o4, l4, o16, l16):
    hp = pl.program_id(0)

    @pl.when(pl.program_id(1) == 0)
    def _():
        _dil_tables(hp, b1, b4, b16)

    head_a = lax.broadcasted_iota(jnp.int32, (1, 2 * SWA_DIM), 1) < SWA_DIM
    u = DIL_UNROLL
    nblk = q_ref.shape[2] * RES // SWA_BLOCK

    def cat(ref, pieces):
        return jnp.concatenate([ref[0, r, rows, :] for r, rows in pieces], axis=0)

    def body1(it, carry):
        tiles, where = [], []
        for j in range(u):
            n = it * u + j
            qrows = pl.ds(pl.multiple_of(n * 8, 8), 8)
            krows = pl.ds(pl.multiple_of(jnp.maximum(n * 8 - 8, 0), 8), 16)
            bias = b1[0] if j else jnp.where(n == 0, b1[1], b1[0])
            tiles.append((cat(q_ref, [(r, qrows) for r in range(RES)]),
                          cat(k_ref, [(r, krows) for r in range(RES)]),
                          cat(v_ref, [(r, krows) for r in range(RES)]), bias))
            where.append(qrows)
        for qrows, (o, lse) in zip(where, _attend(tiles, head_a)):
            for r in range(RES):
                o1[r, qrows, :] = o[8 * r:8 * r + 8]
                l1[r, qrows, :] = lse[8 * r:8 * r + 8]
        return carry

    lax.fori_loop(0, nblk // u, body1, 0)

    def body4(r4, carry):
        tiles = []
        for n in range(4):
            qrows = slice(32 * n, 32 * n + 32)
            k0 = max(32 * n - 32, 0)
            krows = slice(k0, k0 + 64)
            grp = [r4 + 4 * g for g in range(4)]
            tiles.append((cat(q_ref, [(r, qrows) for r in grp]), cat(k_ref, [(r, krows) for r in grp]),
                          cat(v_ref, [(r, krows) for r in grp]), b4[0] if n else b4[1]))
        for n, (o, lse) in enumerate(_attend(tiles, head_a)):
            for g in range(4):
                o4[r4 + 4 * g, 32 * n:32 * n + 32, :] = o[32 * g:32 * g + 32]
                l4[r4 + 4 * g, 32 * n:32 * n + 32, :] = lse[32 * g:32 * g + 32]
        return carry

    lax.fori_loop(0, 4, body4, 0)

    def body16(it, carry):
        rs = [it * u + j for j in range(u)]
        tiles = [(q_ref[0, r], k_ref[0, r], v_ref[0, r], b16[...]) for r in rs]
        for r, (o, lse) in zip(rs, _attend(tiles, head_a)):
            o16[r] = o
            l16[r] = lse
        return carry

    lax.fori_loop(0, RES // u, body16, 0)

    la, lb, lc = l1[...], l4[...], l16[...]
    m = jnp.maximum(jnp.maximum(la, lb), lc)
    wa, wb, wc = jnp.exp(la - m), jnp.exp(lb - m), jnp.exp(lc - m)
    o_ref[0] = (wa * o1[...] + wb * o4[...] + wc * o16[...]) / (wa + wb + wc)


def _dilated(qkvb):
    batch, _, sub, _ = qkvb.shape
    npair = SWA_HEADS // 2
    spec = lambda off: pl.BlockSpec((1, RES, sub, 2 * SWA_DIM), lambda p, b: (b, 0, 0, off + p))
    wide = (2 * SWA_BLOCK, 2 * SWA_BLOCK)
    res = pltpu.VMEM((RES, sub, 2 * SWA_DIM), F32)
    return pl.pallas_call(
        _dil_kernel,
        grid=(npair, batch),
        in_specs=[spec(0), spec(npair), spec(2 * npair)],
        out_specs=spec(0),
        out_shape=jax.ShapeDtypeStruct((batch, RES, sub, SWA_W), F32),
        scratch_shapes=[pltpu.VMEM((2,) + wide, F32), pltpu.VMEM((2,) + wide, F32),
                        pltpu.VMEM((2 * SWA_BLOCK, SWA_BLOCK), F32), res, res, res, res, res, res],
        compiler_params=_params("arbitrary", "arbitrary"),
        name="dilated_attn",
    )(qkvb, qkvb, qkvb)


def _memkv_kernel(m_ref, g_ref, wk_ref, wv_ref, k_ref, v_ref):
    mn = _rms(m_ref[...], g_ref[...]).astype(BF16)
    k_ref[...] = _dot(mn, wk_ref[...]).astype(BF16)
    v_ref[...] = _dot(mn, wv_ref[...]).astype(BF16)


def _memkv(mem, g, wk, wv):
    t = mem.shape[0]
    tm = min(TM, t)
    row = pl.BlockSpec((tm, D_MODEL), lambda i: (i, 0))
    return pl.pallas_call(
        _memkv_kernel,
        grid=(t // tm,),
        in_specs=[row, _resident((1, D_MODEL)), _resident((D_MODEL, D_MODEL)),
                  _resident((D_MODEL, D_MODEL))],
        out_specs=[row, row],
        out_shape=[jax.ShapeDtypeStruct((t, D_MODEL), BF16)] * 2,
        compiler_params=_params("parallel"),
        name="mem_kv",
    )(mem, g, wk, wv)


def _outmem_kernel(x_ref, oa_ref, ob_ref, woa_ref, wob_ref, gmix_ref, gpre_ref, wq_ref,
                   k_ref, v_ref, wo_ref, gpost_ref, o_ref, att, obn):
    for l in range(SWA_W // 128):
        for r in range(RES):
            obn[l, pl.ds(r, TM // RES, stride=RES), :] = ob_ref[0, r, :, 128 * l:128 * l + 128]
    ob = jnp.concatenate([obn[l] for l in range(SWA_W // 128)], axis=1)
    mix = _dot(oa_ref[...].astype(BF16), woa_ref[...]) + _dot(ob.astype(BF16), wob_ref[...])
    x = x_ref[...] + _rms(mix, gmix_ref[...])
    hq = _rms(x, gpre_ref[...]).astype(BF16)
    q = (_dot(hq, wq_ref[...]) * (MEM_DIM ** -0.5)).astype(BF16)
    for h in range(MEM_HEADS):
        hl = slice(h * MEM_DIM, (h + 1) * MEM_DIM)
        s = _dot_nt(q[:, hl], k_ref[:, hl])
        m = jnp.max(s, axis=-1, keepdims=True)
        e = jnp.exp(s - m)
        p = e / jnp.sum(e, axis=-1, keepdims=True)
        att[:, hl] = _dot(p.astype(BF16), v_ref[:, hl]).astype(BF16)
    c = _dot(att[...], wo_ref[...])
    o_ref[...] = x + _rms(c, gpost_ref[...])


def _outmem(x, oa, ob, woa, wob, gmix, gpre, wq, kmem, vmem, wo, gpost, seq, n_mem):
    t = x.shape[0]
    per_seq = seq // TM
    row = lambda w: pl.BlockSpec((TM, w), lambda i: (i, 0))
    kv = pl.BlockSpec((n_mem, D_MODEL), lambda i: (i // per_seq, 0))
    return pl.pallas_call(
        _outmem_kernel,
        grid=(t // TM,),
        in_specs=[row(D_MODEL), row(GDN_W),
                  pl.BlockSpec((1, RES, TM // RES, SWA_W), lambda i: (i // per_seq, 0, i % per_seq, 0)),
                  _resident((GDN_W, D_MODEL)), _resident((SWA_W, D_MODEL)), _resident((1, D_MODEL)),
                  _resident((1, D_MODEL)), _resident((D_MODEL, D_MODEL)), kv, kv,
                  _resident((D_MODEL, D_MODEL)), _resident((1, D_MODEL))],
        out_specs=row(D_MODEL),
        out_shape=jax.ShapeDtypeStruct((t, D_MODEL), F32),
        scratch_shapes=[pltpu.VMEM((TM, D_MODEL), BF16), pltpu.VMEM((SWA_W // 128, TM, 128), F32)],
        compiler_params=_params("parallel"),
        name="out_mem_attn",
    )(x, oa, ob, woa, wob, gmix, gpre, wq, kmem, vmem, wo, gpost)


def _layer(x, mem, p, batch, seq, n_mem):
    bf = lambda w: w.astype(BF16)
    row = lambda g: g.reshape(1, -1)
    x = _ffn(x, row(p["ffn1_pre_g"]), bf(p["ffn1_w_gate"]), bf(p["ffn1_w_up"]),
             bf(p["ffn1_w_down"]), row(p["ffn1_post_g"]))

    w_in = p["w_in"]
    o_z, o_a, o_b = 3 * GDN_W, 4 * GDN_W, 4 * GDN_W + 2 * GDN_HEADS
    wab = jnp.pad(w_in[:, o_a:o_b], ((0, 0), (0, 128 - 2 * GDN_HEADS)))
    cw = jnp.pad(p["gdn_conv_w"], ((0, 8 - GDN_CONV), (0, 0)))
    qkva, z, abc, qkvb = _proj(x, row(p["mix_pre_g"]), bf(w_in[:, :o_z]), bf(w_in[:, o_z:o_a]),
                               bf(wab), bf(w_in[:, o_b:]), cw, batch, seq)

    pad_lane = lambda v: jnp.pad(v.reshape(1, -1), ((0, 0), (0, 128 - GDN_HEADS)))
    per_seq = lambda a: a.reshape(batch, seq, a.shape[-1])
    o_gdn = _gdn(per_seq(qkva), per_seq(z), per_seq(abc), pad_lane(p["gdn_a_log"]),
                 pad_lane(p["gdn_dt_bias"]), row(p["gdn_norm_g"])).reshape(batch * seq, GDN_W)
    o_dil = _dilated(qkvb)

    kmem, vmem = _memkv(mem, row(p["mem_kv_g"]), bf(p["mem_wk"]), bf(p["mem_wv"]))
    w_out = bf(p["w_out"])
    x = _outmem(x, o_gdn, o_dil, w_out[:GDN_W], w_out[GDN_W:], row(p["mix_post_g"]),
                row(p["mem_pre_g"]), bf(p["mem_wq"]), kmem, vmem, bf(p["mem_wo"]),
                row(p["mem_post_g"]), seq, n_mem)

    return _ffn(x, row(p["ffn2_pre_g"]), bf(p["ffn2_w_gate"]), bf(p["ffn2_w_up"]),
                bf(p["ffn2_w_down"]), row(p["ffn2_post_g"]))


_NAMES = ("ffn1_pre_g", "ffn1_w_gate", "ffn1_w_up", "ffn1_w_down", "ffn1_post_g",
          "mix_pre_g", "w_in", "gdn_conv_w", "gdn_a_log", "gdn_dt_bias", "gdn_norm_g", "w_out",
          "mix_post_g", "mem_pre_g", "mem_kv_g", "mem_wq", "mem_wk", "mem_wv", "mem_wo",
          "mem_post_g", "ffn2_pre_g", "ffn2_w_gate", "ffn2_w_up", "ffn2_w_down", "ffn2_post_g")


def kernel(x, mem, ffn1_pre_g, ffn1_w_gate, ffn1_w_up, ffn1_w_down, ffn1_post_g, mix_pre_g, w_in, gdn_conv_w, gdn_a_log, gdn_dt_bias, gdn_norm_g, w_out, mix_post_g, mem_pre_g, mem_kv_g, mem_wq, mem_wk, mem_wv, mem_wo, mem_post_g, ffn2_pre_g, ffn2_w_gate, ffn2_w_up, ffn2_w_down, ffn2_post_g):
    stacked = dict(zip(_NAMES, (ffn1_pre_g, ffn1_w_gate, ffn1_w_up, ffn1_w_down, ffn1_post_g,
                                mix_pre_g, w_in, gdn_conv_w, gdn_a_log, gdn_dt_bias, gdn_norm_g, w_out,
                                mix_post_g, mem_pre_g, mem_kv_g, mem_wq, mem_wk, mem_wv, mem_wo,
                                mem_post_g, ffn2_pre_g, ffn2_w_gate, ffn2_w_up, ffn2_w_down,
                                ffn2_post_g)))
    batch, seq, _ = x.shape
    n_mem = mem.shape[1]
    xf = x.reshape(batch * seq, D_MODEL)
    memf = mem.reshape(batch * n_mem, D_MODEL)
    for l in range(ffn1_pre_g.shape[0]):
        xf = _layer(xf, memf, {k: v[l] for k, v in stacked.items()}, batch, seq, n_mem)
    return xf.reshape(batch, seq, D_MODEL)
```

```python
import functools

import jax
import jax.numpy as jnp
from jax import lax
from jax.experimental import pallas as pl
from jax.experimental.pallas import tpu as pltpu

F32 = jnp.float32
BF16 = jnp.bfloat16

D_MODEL = 1024
D_FF = 2816
EPS = 1e-6
GDN_HEADS = 4
GDN_DIM = 128
GDN_W = GDN_HEADS * GDN_DIM
GDN_CONV = 4
CHUNK = 64
SWA_HEADS = 8
SWA_DIM = 64
SWA_W = SWA_HEADS * SWA_DIM
SWA_BLOCK = 128
DILATIONS = (1, 4, 16)
RES = 16
MEM_HEADS = 4
MEM_DIM = D_MODEL // MEM_HEADS
NEG = -1e30

V7X_VMEM_BYTES = 64 * 1024 * 1024
VMEM_LIMIT = V7X_VMEM_BYTES - 8 * 1024 * 1024

TM = 512
HALO = 16
FFN_FC = 256
PROJ_TILE = 256
OUT_SUB = 2
GDN_GB = 4
DIL_UNROLL = 4


def _rms(x, g):
    return x * lax.rsqrt(jnp.mean(x * x, axis=-1, keepdims=True) + EPS) * g


def _dot(a, b):
    return jnp.dot(a, b, preferred_element_type=F32)


def _dot_nt(a, b):
    return lax.dot_general(a, b, (((1,), (1,)), ((), ())), preferred_element_type=F32)


def _dot_tn(a, b):
    return lax.dot_general(a, b, (((0,), (0,)), ((), ())), preferred_element_type=F32)


def _resident(shape):
    return pl.BlockSpec(shape, lambda *_: (0,) * len(shape), pipeline_mode=pl.Buffered(1))


def _params(*sem):
    return pltpu.CompilerParams(dimension_semantics=sem, vmem_limit_bytes=VMEM_LIMIT)


def _ffn_kernel(x_ref, gpre_ref, wg_ref, wu_ref, wd_ref, gpost_ref, o_ref, a_ref):
    x = x_ref[...]
    xn = _rms(x, gpre_ref[...]).astype(BF16)
    for c in range(D_FF // FFN_FC):
        sl = slice(c * FFN_FC, (c + 1) * FFN_FC)
        g = _dot(xn, wg_ref[:, sl])
        u = _dot(xn, wu_ref[:, sl])
        a_ref[:, sl] = (g * jax.nn.sigmoid(g) * u).astype(BF16)
    f = _dot(a_ref[...], wd_ref[...])
    o_ref[...] = x + 0.5 * _rms(f, gpost_ref[...])


def _ffn(x, gpre, wg, wu, wd, gpost):
    t = x.shape[0]
    row = pl.BlockSpec((TM, D_MODEL), lambda i: (i, 0))
    return pl.pallas_call(
        _ffn_kernel,
        grid=(t // TM,),
        in_specs=[row, _resident((1, D_MODEL)), _resident((D_MODEL, D_FF)),
                  _resident((D_MODEL, D_FF)), _resident((D_FF, D_MODEL)),
                  _resident((1, D_MODEL))],
        out_specs=row,
        out_shape=jax.ShapeDtypeStruct((t, D_MODEL), F32),
        scratch_shapes=[pltpu.VMEM((TM, D_FF), BF16)],
        compiler_params=_params("parallel"),
        name="ffn",
    )(x, gpre, wg, wu, wd, gpost)


def _proj_kernel(x_ref, xh_ref, g_ref, wa_ref, wz_ref, wab_ref, wb_ref, cw_ref,
                 qkva_ref, z_ref, abc_ref, qkvb_ref, rb, *, per_seq):
    g = g_ref[...]
    h = _rms(x_ref[...], g).astype(BF16)
    keep = jnp.where(pl.program_id(0) % per_seq == 0, 0.0, 1.0)
    hx = jnp.concatenate([(_rms(xh_ref[...], g) * keep).astype(BF16), h], axis=0)

    def conv_tile(t, xa):
        for hd in range(PROJ_TILE // GDN_DIM):
            lanes = slice(PROJ_TILE * t + hd * GDN_DIM, PROJ_TILE * t + (hd + 1) * GDN_DIM)
            xh = xa[:, hd * GDN_DIM:(hd + 1) * GDN_DIM]
            y = cw_ref[0:1, lanes] * xh
            for j in range(1, GDN_CONV):
                y = pltpu.roll(y, 1, axis=0) + cw_ref[j:j + 1, lanes] * xh
            y = y[HALO:]
            y = y * jax.nn.sigmoid(y)
            if lanes.start < 2 * GDN_W:
                scale = GDN_DIM ** -0.5 if lanes.start < GDN_W else 1.0
                y = y * (lax.rsqrt(jnp.sum(y * y, axis=-1, keepdims=True) + EPS) * scale)
            qkva_ref[:, lanes] = y

    def attn_tile(t):
        yb = _dot(h, wb_ref[:, PROJ_TILE * t:PROJ_TILE * (t + 1)])
        for half in range(PROJ_TILE // 128):
            l = t * (PROJ_TILE // 128) + half
            rb[l] = yb[:, 128 * half:128 * half + 128]
            for r in range(RES):
                qkvb_ref[0, r, :, 128 * l:128 * l + 128] = rb[l, pl.ds(r, TM // RES, stride=RES), :]

    n_a = 3 * GDN_W // PROJ_TILE
    prev = None
    for t in range(n_a):
        xa = _dot(hx, wa_ref[:, PROJ_TILE * t:PROJ_TILE * (t + 1)])
        attn_tile(t)
        if t:
            conv_tile(t - 1, prev)
        prev = xa
    z_ref[...] = _dot(h, wz_ref[...])
    abc_ref[...] = _dot(h, wab_ref[...])
    conv_tile(n_a - 1, prev)


def _proj(x, g, wa, wz, wab, wb, cw, batch, seq):
    t = x.shape[0]
    per_seq = seq // TM
    row = lambda w: pl.BlockSpec((TM, w), lambda i: (i, 0))
    halo = pl.BlockSpec((HALO, D_MODEL), lambda i: (jnp.maximum(i * (TM // HALO) - 1, 0), 0))
    return pl.pallas_call(
        functools.partial(_proj_kernel, per_seq=per_seq),
        grid=(t // TM,),
        in_specs=[row(D_MODEL), halo, _resident((1, D_MODEL)), _resident(wa.shape),
                  _resident(wz.shape), _resident(wab.shape), _resident(wb.shape),
                  _resident(cw.shape)],
        out_specs=[row(3 * GDN_W), row(GDN_W), row(128),
                   pl.BlockSpec((1, RES, TM // RES, 3 * SWA_W),
                                lambda i: (i // per_seq, 0, i % per_seq, 0))],
        out_shape=[jax.ShapeDtypeStruct((t, 3 * GDN_W), F32),
                   jax.ShapeDtypeStruct((t, GDN_W), F32),
                   jax.ShapeDtypeStruct((t, 128), F32),
                   jax.ShapeDtypeStruct((batch, RES, seq // RES, 3 * SWA_W), F32)],
        scratch_shapes=[pltpu.VMEM((3 * SWA_W // 128, TM, 128), F32)],
        compiler_params=_params("parallel"),
        name="mixer_proj",
    )(x, x, g, wa, wz, wab, wb, cw)


def _softplus(x):
    return jnp.maximum(x, 0.0) + jnp.log(1.0 + jnp.exp(-jnp.abs(x)))


def _level_masks():
    row = lax.broadcasted_iota(jnp.int32, (CHUNK, CHUNK), 0)
    col = lax.broadcasted_iota(jnp.int32, (CHUNK, CHUNK), 1)
    sels = []
    for lb in range(CHUNK.bit_length() - 1):
        sels.append(((row >> (lb + 1)) == (col >> (lb + 1))) & (((row >> lb) & 1) == 1)
                    & (((col >> lb) & 1) == 0))
    return row, col, sels


def _unit_lower_inverse(a_list, eye, sels):
    ts = [eye - jnp.where(sels[0], a, 0.0) for a in a_list]
    for sel in sels[1:]:
        tb = [t.astype(BF16) for t in ts]
        ps = [_dot(jnp.where(sel, a, 0.0).astype(BF16), t) for a, t in zip(a_list, tb)]
        ts = [t - _dot(t16, p.astype(BF16)) for t, t16, p in zip(ts, tb, ps)]
    return ts


def _gdn_kernel(x_ref, z_ref, abc_ref, alc_ref, dtc_ref, ng_ref, o_ref, st):
    nbatch = x_ref.shape[0]

    @pl.when(pl.program_id(0) == 0)
    def _():
        st[...] = jnp.zeros_like(st)

    row, col, sels = _level_masks()
    tril = col <= row
    strict = col < row
    eye = jnp.where(row == col, 1.0, 0.0).astype(F32)
    lower = jnp.where(tril, 1.0, 0.0).astype(F32)
    upper = jnp.where(row <= col, 1.0, 0.0).astype(F32)
    hi = lax.Precision.HIGHEST

    def group(gi, carry):
        items = []
        for bb in range(GDN_GB):
            b = gi * GDN_GB + bb
            abc = abc_ref[b]
            g_col = -jnp.exp(alc_ref[...]) * _softplus(abc + dtc_ref[...])
            gc_col = jnp.dot(lower, g_col, precision=hi, preferred_element_type=F32)
            gc_row = lax.dot_general(g_col, upper, (((0,), (0,)), ((), ())), precision=hi,
                                     preferred_element_type=F32)
            beta_all = jax.nn.sigmoid(abc)
            for h in range(GDN_HEADS):
                items.append(dict(
                    b=b, h=h,
                    q=x_ref[b, :, h * GDN_DIM:(h + 1) * GDN_DIM],
                    k=x_ref[b, :, GDN_W + h * GDN_DIM:GDN_W + (h + 1) * GDN_DIM],
                    v=x_ref[b, :, 2 * GDN_W + h * GDN_DIM:2 * GDN_W + (h + 1) * GDN_DIM],
                    gcc=gc_col[:, h:h + 1], gcr=gc_row[h:h + 1, :],
                    gl=gc_col[CHUNK - 1:CHUNK, h:h + 1], beta=beta_all[:, 4 + h:5 + h]))

        kqs = []
        for it in items:
            it["kb"] = it["k"] * it["beta"]
            kqs.append(_dot_nt(jnp.concatenate([it["kb"], it["q"]], axis=0).astype(BF16),
                               it["k"].astype(BF16)))
        a_list = []
        for it, kq in zip(items, kqs):
            decay = jnp.exp(jnp.where(tril, it["gcc"] - it["gcr"], NEG))
            a_list.append(jnp.where(strict, kq[:CHUNK] * decay, 0.0))
            it["qk"] = jnp.where(tril, kq[CHUNK:] * decay, 0.0).astype(BF16)
        ts = _unit_lower_inverse(a_list, eye, sels)
        uws = []
        for it, t in zip(items, ts):
            eg = jnp.exp(it["gcc"])
            rhs = jnp.concatenate([it["v"] * it["beta"], it["kb"] * eg], axis=1).astype(BF16)
            uws.append(_dot(t.astype(BF16), rhs))
            it["qd"] = (it["q"] * eg).astype(BF16)
            it["kd"] = (it["k"] * jnp.exp(it["gl"] - it["gcc"])).astype(BF16)

        states, wss = [], []
        for it, uw in zip(items, uws):
            s = st[it["b"] * GDN_HEADS + it["h"]]
            states.append(s)
            lhs = jnp.concatenate([uw[:, GDN_DIM:].astype(BF16), it["qd"]], axis=0)
            wss.append(_dot(lhs, s.astype(BF16)))
        for it, uw, s, ws in zip(items, uws, states, wss):
            vb = (uw[:, :GDN_DIM] - ws[:CHUNK]).astype(BF16)
            o = ws[CHUNK:] + _dot(it["qk"], vb)
            st[it["b"] * GDN_HEADS + it["h"]] = s * jnp.exp(it["gl"]) + _dot_tn(it["kd"], vb)
            hl = slice(it["h"] * GDN_DIM, (it["h"] + 1) * GDN_DIM)
            zz = z_ref[it["b"], :, hl]
            o = o * lax.rsqrt(jnp.mean(o * o, axis=-1, keepdims=True) + EPS)
            o_ref[it["b"], :, hl] = o * ng_ref[...] * (zz * jax.nn.sigmoid(zz))
        return carry

    lax.fori_loop(0, nbatch // GDN_GB, group, 0)


def _gdn(qkva, z, abc, alc, dtc, ng):
    batch, seq, _ = qkva.shape
    blk = lambda w: pl.BlockSpec((batch, CHUNK, w), lambda c: (0, c, 0))
    const = lambda shape: pl.BlockSpec(shape, lambda c: (0, 0))
    return pl.pallas_call(
        _gdn_kernel,
        grid=(seq // CHUNK,),
        in_specs=[blk(3 * GDN_W), blk(GDN_W), blk(128),
                  const((1, 128)), const((1, 128)), const((1, GDN_DIM))],
        out_specs=blk(GDN_W),
        out_shape=jax.ShapeDtypeStruct((batch, seq, GDN_W), F32),
        scratch_shapes=[pltpu.VMEM((batch * GDN_HEADS, GDN_DIM, GDN_DIM), F32)],
        compiler_params=_params("arbitrary"),
        name="gdn",
    )(qkva, z, abc, alc, dtc, ng)


def _dil_bias(delta, dilation, slopes):
    valid = (delta >= 0) & (delta <= SWA_BLOCK)
    df = delta.astype(F32)
    return jnp.concatenate([jnp.where(valid, -(s * float(dilation)) * df, NEG) for s in slopes], axis=0)


def _dil_tables(hp, b1, b4, b16):
    def iota(shape, dim):
        return lax.broadcasted_iota(jnp.int32, shape, dim)

    def slope(head, shape):
        e = jnp.full(shape, 126, jnp.int32) - head
        return lax.bitcast_convert_type(lax.shift_left(e, jnp.full_like(e, 23)), F32)

    wide = (SWA_BLOCK, 2 * SWA_BLOCK)
    sl = [slope(2 * hp, wide), slope(2 * hp + 1, wide)]
    p, c = iota(wide, 0), iota(wide, 1)
    d = 16 * ((p & 7) - (c & 15)) + ((p >> 3) - (c >> 4))
    b1[0] = _dil_bias(d + 16 * 8, 1, sl)
    b1[1] = _dil_bias(d, 1, sl)
    d = 4 * ((p & 31) - (c & 63)) + ((p >> 5) - (c >> 6))
    b4[0] = _dil_bias(d + 4 * 32, 4, sl)
    b4[1] = _dil_bias(d, 4, sl)
    sq = (SWA_BLOCK, SWA_BLOCK)
    b16[...] = _dil_bias(iota(sq, 0) - iota(sq, 1), 16, [slope(2 * hp, sq), slope(2 * hp + 1, sq)])


def _attend(tiles, head_a):
    scores = []
    for q, k, v, bias in tiles:
        qa = jnp.where(head_a, q, 0.0)
        qs = (jnp.concatenate([qa, q - qa], axis=0) * (SWA_DIM ** -0.5)).astype(BF16)
        scores.append(_dot_nt(qs, k.astype(BF16)) + bias)
    probs = []
    for s in scores:
        m = jnp.max(s, axis=-1, keepdims=True)
        e = jnp.exp(s - m)
        l = jnp.sum(e, axis=-1, keepdims=True)
        probs.append((e.astype(BF16), 1.0 / l, m + jnp.log(l)))
    outs = []
    for (q, k, v, bias), (e, inv, lse) in zip(tiles, probs):
        o = _dot(e, v.astype(BF16)) * inv
        outs.append((jnp.where(head_a, o[:SWA_BLOCK], o[SWA_BLOCK:]),
                     jnp.where(head_a, lse[:SWA_BLOCK], lse[SWA_BLOCK:])))
    return outs


def _dil_kernel(q_ref, k_ref, v_ref, o_ref, b1, b4, b16, o1, l1, o4, l4, o16, l16):
    hp = pl.program_id(0)

    @pl.when(pl.program_id(1) == 0)
    def _():
        _dil_tables(hp, b1, b4, b16)

    head_a = lax.broadcasted_iota(jnp.int32, (1, 2 * SWA_DIM), 1) < SWA_DIM
    u = DIL_UNROLL
    nblk = q_ref.shape[2] * RES // SWA_BLOCK

    def cat(ref, pieces):
        return jnp.concatenate([ref[0, r, rows, :] for r, rows in pieces], axis=0)

    def body1(it, carry):
        tiles, where = [], []
        for j in range(u):
            n = it * u + j
            qrows = pl.ds(pl.multiple_of(n * 8, 8), 8)
            krows = pl.ds(pl.multiple_of(jnp.maximum(n * 8 - 8, 0), 8), 16)
            bias = b1[0] if j else jnp.where(n == 0, b1[1], b1[0])
            tiles.append((cat(q_ref, [(r, qrows) for r in range(RES)]),
                          cat(k_ref, [(r, krows) for r in range(RES)]),
                          cat(v_ref, [(r, krows) for r in range(RES)]), bias))
            where.append(qrows)
        for qrows, (o, lse) in zip(where, _attend(tiles, head_a)):
            for r in range(RES):
                o1[r, qrows, :] = o[8 * r:8 * r + 8]
                l1[r, qrows, :] = lse[8 * r:8 * r + 8]
        return carry

    lax.fori_loop(0, nblk // u, body1, 0)

    def body4(r4, carry):
        tiles = []
        for n in range(4):
            qrows = slice(32 * n, 32 * n + 32)
            k0 = max(32 * n - 32, 0)
            krows = slice(k0, k0 + 64)
            grp = [r4 + 4 * g for g in range(4)]
            tiles.append((cat(q_ref, [(r, qrows) for r in grp]), cat(k_ref, [(r, krows) for r in grp]),
                          cat(v_ref, [(r, krows) for r in grp]), b4[0] if n else b4[1]))
        for n, (o, lse) in enumerate(_attend(tiles, head_a)):
            for g in range(4):
                o4[r4 + 4 * g, 32 * n:32 * n + 32, :] = o[32 * g:32 * g + 32]
                l4[r4 + 4 * g, 32 * n:32 * n + 32, :] = lse[32 * g:32 * g + 32]
        return carry

    lax.fori_loop(0, 4, body4, 0)

    def body16(it, carry):
        rs = [it * u + j for j in range(u)]
        tiles = [(q_ref[0, r], k_ref[0, r], v_ref[0, r], b16[...]) for r in rs]
        for r, (o, lse) in zip(rs, _attend(tiles, head_a)):
            o16[r] = o
            l16[r] = lse
        return carry

    lax.fori_loop(0, RES // u, body16, 0)

    la, lb, lc = l1[...], l4[...], l16[...]
    m = jnp.maximum(jnp.maximum(la, lb), lc)
    wa, wb, wc = jnp.exp(la - m), jnp.exp(lb - m), jnp.exp(lc - m)
    o_ref[0] = (wa * o1[...] + wb * o4[...] + wc * o16[...]) / (wa + wb + wc)


def _dilated(qkvb):
    batch, _, sub, _ = qkvb.shape
    npair = SWA_HEADS // 2
    spec = lambda off: pl.BlockSpec((1, RES, sub, 2 * SWA_DIM), lambda p, b: (b, 0, 0, off + p))
    wide = (2 * SWA_BLOCK, 2 * SWA_BLOCK)
    res = pltpu.VMEM((RES, sub, 2 * SWA_DIM), F32)
    return pl.pallas_call(
        _dil_kernel,
        grid=(npair, batch),
        in_specs=[spec(0), spec(npair), spec(2 * npair)],
        out_specs=spec(0),
        out_shape=jax.ShapeDtypeStruct((batch, RES, sub, SWA_W), F32),
        scratch_shapes=[pltpu.VMEM((2,) + wide, F32), pltpu.VMEM((2,) + wide, F32),
                        pltpu.VMEM((2 * SWA_BLOCK, SWA_BLOCK), F32), res, res, res, res, res, res],
        compiler_params=_params("arbitrary", "arbitrary"),
        name="dilated_attn",
    )(qkvb, qkvb, qkvb)


def _memkv_kernel(m_ref, g_ref, wk_ref, wv_ref, k_ref, v_ref):
    mn = _rms(m_ref[...], g_ref[...]).astype(BF16)
    k_ref[...] = _dot(mn, wk_ref[...]).astype(BF16)
    v_ref[...] = _dot(mn, wv_ref[...]).astype(BF16)


def _memkv(mem, g, wk, wv):
    t = mem.shape[0]
    tm = min(TM, t)
    row = pl.BlockSpec((tm, D_MODEL), lambda i: (i, 0))
    return pl.pallas_call(
        _memkv_kernel,
        grid=(t // tm,),
        in_specs=[row, _resident((1, D_MODEL)), _resident((D_MODEL, D_MODEL)),
                  _resident((D_MODEL, D_MODEL))],
        out_specs=[row, row],
        out_shape=[jax.ShapeDtypeStruct((t, D_MODEL), BF16)] * 2,
        compiler_params=_params("parallel"),
        name="mem_kv",
    )(mem, g, wk, wv)


def _outmem_kernel(x_ref, oa_ref, ob_ref, woa_ref, wob_ref, gmix_ref, gpre_ref, wq_ref,
                   k_ref, v_ref, wo_ref, gpost_ref, o_ref, att, obn):
    rs = TM // OUT_SUB
    subs = [slice(i * rs, (i + 1) * rs) for i in range(OUT_SUB)]
    mixes = [_dot(oa_ref[s, :].astype(BF16), woa_ref[...]) for s in subs]
    for l in range(SWA_W // 128):
        for r in range(RES):
            obn[l, pl.ds(r, TM // RES, stride=RES), :] = ob_ref[0, r, :, 128 * l:128 * l + 128]
    xs = []
    for s, mix in zip(subs, mixes):
        ob = jnp.concatenate([obn[l, s, :] for l in range(SWA_W // 128)], axis=1).astype(BF16)
        xs.append(x_ref[s, :] + _rms(mix + _dot(ob, wob_ref[...]), gmix_ref[...]))
    qs = [(_dot(_rms(x, gpre_ref[...]).astype(BF16), wq_ref[...]) * (MEM_DIM ** -0.5)).astype(BF16)
          for x in xs]
    for h in range(MEM_HEADS):
        hl = slice(h * MEM_DIM, (h + 1) * MEM_DIM)
        scores = [_dot_nt(q[:, hl], k_ref[:, hl]) for q in qs]
        for s, sc in zip(subs, scores):
            e = jnp.exp(sc - jnp.max(sc, axis=-1, keepdims=True))
            p = e / jnp.sum(e, axis=-1, keepdims=True)
            att[s, hl] = _dot(p.astype(BF16), v_ref[:, hl]).astype(BF16)
    cs = [_dot(att[s, :], wo_ref[...]) for s in subs]
    for s, x, c in zip(subs, xs, cs):
        o_ref[s, :] = x + _rms(c, gpost_ref[...])


def _outmem(x, oa, ob, woa, wob, gmix, gpre, wq, kmem, vmem, wo, gpost, seq, n_mem):
    t = x.shape[0]
    per_seq = seq // TM
    row = lambda w: pl.BlockSpec((TM, w), lambda i: (i, 0))
    kv = pl.BlockSpec((n_mem, D_MODEL), lambda i: (i // per_seq, 0))
    return pl.pallas_call(
        _outmem_kernel,
        grid=(t // TM,),
        in_specs=[row(D_MODEL), row(GDN_W),
                  pl.BlockSpec((1, RES, TM // RES, SWA_W), lambda i: (i // per_seq, 0, i % per_seq, 0)),
                  _resident((GDN_W, D_MODEL)), _resident((SWA_W, D_MODEL)), _resident((1, D_MODEL)),
                  _resident((1, D_MODEL)), _resident((D_MODEL, D_MODEL)), kv, kv,
                  _resident((D_MODEL, D_MODEL)), _resident((1, D_MODEL))],
        out_specs=row(D_MODEL),
        out_shape=jax.ShapeDtypeStruct((t, D_MODEL), F32),
        scratch_shapes=[pltpu.VMEM((TM, D_MODEL), BF16), pltpu.VMEM((SWA_W // 128, TM, 128), F32)],
        compiler_params=_params("parallel"),
        name="out_mem_attn",
    )(x, oa, ob, woa, wob, gmix, gpre, wq, kmem, vmem, wo, gpost)


def _layer(x, mem, p, batch, seq, n_mem):
    bf = lambda w: w.astype(BF16)
    row = lambda g: g.reshape(1, -1)
    x = _ffn(x, row(p["ffn1_pre_g"]), bf(p["ffn1_w_gate"]), bf(p["ffn1_w_up"]),
             bf(p["ffn1_w_down"]), row(p["ffn1_post_g"]))

    w_in = p["w_in"]
    o_z, o_a, o_b = 3 * GDN_W, 4 * GDN_W, 4 * GDN_W + 2 * GDN_HEADS
    wab = jnp.pad(w_in[:, o_a:o_b], ((0, 0), (0, 128 - 2 * GDN_HEADS)))
    cw = jnp.pad(p["gdn_conv_w"], ((0, 8 - GDN_CONV), (0, 0)))
    qkva, z, abc, qkvb = _proj(x, row(p["mix_pre_g"]), bf(w_in[:, :o_z]), bf(w_in[:, o_z:o_a]),
                               bf(wab), bf(w_in[:, o_b:]), cw, batch, seq)

    pad_lane = lambda v: jnp.pad(v.reshape(1, -1), ((0, 0), (0, 128 - GDN_HEADS)))
    per_seq = lambda a: a.reshape(batch, seq, a.shape[-1])
    o_gdn = _gdn(per_seq(qkva), per_seq(z), per_seq(abc), pad_lane(p["gdn_a_log"]),
                 pad_lane(p["gdn_dt_bias"]), row(p["gdn_norm_g"])).reshape(batch * seq, GDN_W)
    o_dil = _dilated(qkvb)

    kmem, vmem = _memkv(mem, row(p["mem_kv_g"]), bf(p["mem_wk"]), bf(p["mem_wv"]))
    w_out = bf(p["w_out"])
    x = _outmem(x, o_gdn, o_dil, w_out[:GDN_W], w_out[GDN_W:], row(p["mix_post_g"]),
                row(p["mem_pre_g"]), bf(p["mem_wq"]), kmem, vmem, bf(p["mem_wo"]),
                row(p["mem_post_g"]), seq, n_mem)

    return _ffn(x, row(p["ffn2_pre_g"]), bf(p["ffn2_w_gate"]), bf(p["ffn2_w_up"]),
                bf(p["ffn2_w_down"]), row(p["ffn2_post_g"]))


_NAMES = ("ffn1_pre_g", "ffn1_w_gate", "ffn1_w_up", "ffn1_w_down", "ffn1_post_g",
          "mix_pre_g", "w_in", "gdn_conv_w", "gdn_a_log", "gdn_dt_bias", "gdn_norm_g", "w_out",
          "mix_post_g", "mem_pre_g", "mem_kv_g", "mem_wq", "mem_wk", "mem_wv", "mem_wo",
          "mem_post_g", "ffn2_pre_g", "ffn2_w_gate", "ffn2_w_up", "ffn2_w_down", "ffn2_post_g")


def kernel(x, mem, ffn1_pre_g, ffn1_w_gate, ffn1_w_up, ffn1_w_down, ffn1_post_g, mix_pre_g, w_in, gdn_conv_w, gdn_a_log, gdn_dt_bias, gdn_norm_g, w_out, mix_post_g, mem_pre_g, mem_kv_g, mem_wq, mem_wk, mem_wv, mem_wo, mem_post_g, ffn2_pre_g, ffn2_w_gate, ffn2_w_up, ffn2_w_down, ffn2_post_g):
    stacked = dict(zip(_NAMES, (ffn1_pre_g, ffn1_w_gate, ffn1_w_up, ffn1_w_down, ffn1_post_g,
                                mix_pre_g, w_in, gdn_conv_w, gdn_a_log, gdn_dt_bias, gdn_norm_g, w_out,
                                mix_post_g, mem_pre_g, mem_kv_g, mem_wq, mem_wk, mem_wv, mem_wo,
                                mem_post_g, ffn2_pre_g, ffn2_w_gate, ffn2_w_up, ffn2_w_down,
                                ffn2_post_g)))
    batch, seq, _ = x.shape
    n_mem = mem.shape[1]
    xf = x.reshape(batch * seq, D_MODEL)
    memf = mem.reshape(batch * n_mem, D_MODEL)
    for l in range(ffn1_pre_g.shape[0]):
        xf = _layer(xf, memf, {k: v[l] for k, v in stacked.items()}, batch, seq, n_mem)
    return xf.reshape(batch, seq, D_MODEL)
```

```python
import functools

import jax
import jax.numpy as jnp
from jax import lax
from jax.experimental import pallas as pl
from jax.experimental.pallas import tpu as pltpu

F32 = jnp.float32
BF16 = jnp.bfloat16

D_MODEL = 1024
D_FF = 2816
EPS = 1e-6
GDN_HEADS = 4
GDN_DIM = 128
GDN_W = GDN_HEADS * GDN_DIM
GDN_CONV = 4
CHUNK = 64
SWA_HEADS = 8
SWA_DIM = 64
SWA_W = SWA_HEADS * SWA_DIM
SWA_BLOCK = 128
DILATIONS = (1, 4, 16)
RES = 16
MEM_HEADS = 4
MEM_DIM = D_MODEL // MEM_HEADS
NEG = -1e30
LOG2E = 1.4426950408889634

V7X_VMEM_BYTES = 64 * 1024 * 1024
VMEM_LIMIT = V7X_VMEM_BYTES - 8 * 1024 * 1024

TM = 512
HALO = 16
FFN_FC = 256
FFN_SUB = 2
PROJ_TILE = 256
OUT_SUB = 2
GDN_GB = 8
DIL_UNROLL = 8


def _rms(x, g):
    return x * lax.rsqrt(jnp.mean(x * x, axis=-1, keepdims=True) + EPS) * g


def _dot(a, b):
    return jnp.dot(a, b, preferred_element_type=F32)


def _dot_nt(a, b):
    return lax.dot_general(a, b, (((1,), (1,)), ((), ())), preferred_element_type=F32)


def _dot_tn(a, b):
    return lax.dot_general(a, b, (((0,), (0,)), ((), ())), preferred_element_type=F32)


def _resident(shape):
    return pl.BlockSpec(shape, lambda *_: (0,) * len(shape), pipeline_mode=pl.Buffered(1))


def _params(*sem):
    return pltpu.CompilerParams(dimension_semantics=sem, vmem_limit_bytes=VMEM_LIMIT)


def _ffn_kernel(x_ref, gpre_ref, wg_ref, wu_ref, wd_ref, gpost_ref, o_ref, a_ref):
    rs = TM // FFN_SUB
    subs = [slice(i * rs, (i + 1) * rs) for i in range(FFN_SUB)]
    xns = [_rms(x_ref[s, :], gpre_ref[...]).astype(BF16) for s in subs]
    for c in range(D_FF // FFN_FC):
        sl = slice(c * FFN_FC, (c + 1) * FFN_FC)
        for s, xn in zip(subs, xns):
            g = _dot(xn, wg_ref[:, sl])
            u = _dot(xn, wu_ref[:, sl])
            a_ref[s, sl] = (g * jax.nn.sigmoid(g) * u).astype(BF16)
    for s in subs:
        f = _dot(a_ref[s, :], wd_ref[...])
        o_ref[s, :] = x_ref[s, :] + 0.5 * _rms(f, gpost_ref[...])


def _ffn(x, gpre, wg, wu, wd, gpost):
    t = x.shape[0]
    row = pl.BlockSpec((TM, D_MODEL), lambda i: (i, 0))
    return pl.pallas_call(
        _ffn_kernel,
        grid=(t // TM,),
        in_specs=[row, _resident((1, D_MODEL)), _resident((D_MODEL, D_FF)),
                  _resident((D_MODEL, D_FF)), _resident((D_FF, D_MODEL)),
                  _resident((1, D_MODEL))],
        out_specs=row,
        out_shape=jax.ShapeDtypeStruct((t, D_MODEL), F32),
        scratch_shapes=[pltpu.VMEM((TM, D_FF), BF16)],
        compiler_params=_params("parallel"),
        name="ffn",
    )(x, gpre, wg, wu, wd, gpost)


def _proj_kernel(x_ref, xh_ref, g_ref, wa_ref, wz_ref, wab_ref, wb_ref, cw_ref,
                 qkva_ref, z_ref, abc_ref, qkvb_ref, rb, *, per_seq):
    g = g_ref[...]
    h = _rms(x_ref[...], g).astype(BF16)
    keep = jnp.where(pl.program_id(0) % per_seq == 0, 0.0, 1.0)
    hx = jnp.concatenate([(_rms(xh_ref[...], g) * keep).astype(BF16), h], axis=0)

    def conv_tile(t, xa):
        for hd in range(PROJ_TILE // GDN_DIM):
            lanes = slice(PROJ_TILE * t + hd * GDN_DIM, PROJ_TILE * t + (hd + 1) * GDN_DIM)
            xh = xa[:, hd * GDN_DIM:(hd + 1) * GDN_DIM]
            y = cw_ref[0:1, lanes] * xh
            for j in range(1, GDN_CONV):
                y = pltpu.roll(y, 1, axis=0) + cw_ref[j:j + 1, lanes] * xh
            y = y[HALO:]
            y = y * jax.nn.sigmoid(y)
            if lanes.start < 2 * GDN_W:
                scale = GDN_DIM ** -0.5 if lanes.start < GDN_W else 1.0
                y = y * (lax.rsqrt(jnp.sum(y * y, axis=-1, keepdims=True) + EPS) * scale)
            qkva_ref[:, lanes] = y

    def attn_tile(t):
        yb = _dot(h, wb_ref[:, PROJ_TILE * t:PROJ_TILE * (t + 1)])
        for half in range(PROJ_TILE // 128):
            l = t * (PROJ_TILE // 128) + half
            rb[l] = yb[:, 128 * half:128 * half + 128]
            for r in range(RES):
                qkvb_ref[0, r, :, 128 * l:128 * l + 128] = rb[l, pl.ds(r, TM // RES, stride=RES), :]

    n_a = 3 * GDN_W // PROJ_TILE
    prev = None
    for t in range(n_a):
        xa = _dot(hx, wa_ref[:, PROJ_TILE * t:PROJ_TILE * (t + 1)])
        attn_tile(t)
        if t:
            conv_tile(t - 1, prev)
        prev = xa
    z_ref[...] = _dot(h, wz_ref[...])
    abc_ref[...] = _dot(h, wab_ref[...])
    conv_tile(n_a - 1, prev)


def _proj(x, g, wa, wz, wab, wb, cw, batch, seq):
    t = x.shape[0]
    per_seq = seq // TM
    row = lambda w: pl.BlockSpec((TM, w), lambda i: (i, 0))
    halo = pl.BlockSpec((HALO, D_MODEL), lambda i: (jnp.maximum(i * (TM // HALO) - 1, 0), 0))
    return pl.pallas_call(
        functools.partial(_proj_kernel, per_seq=per_seq),
        grid=(t // TM,),
        in_specs=[row(D_MODEL), halo, _resident((1, D_MODEL)), _resident(wa.shape),
                  _resident(wz.shape), _resident(wab.shape), _resident(wb.shape),
                  _resident(cw.shape)],
        out_specs=[row(3 * GDN_W), row(GDN_W), row(128),
                   pl.BlockSpec((1, RES, TM // RES, 3 * SWA_W),
                                lambda i: (i // per_seq, 0, i % per_seq, 0))],
        out_shape=[jax.ShapeDtypeStruct((t, 3 * GDN_W), F32),
                   jax.ShapeDtypeStruct((t, GDN_W), F32),
                   jax.ShapeDtypeStruct((t, 128), F32),
                   jax.ShapeDtypeStruct((batch, RES, seq // RES, 3 * SWA_W), F32)],
        scratch_shapes=[pltpu.VMEM((3 * SWA_W // 128, TM, 128), F32)],
        compiler_params=_params("parallel"),
        name="mixer_proj",
    )(x, x, g, wa, wz, wab, wb, cw)


def _softplus(x):
    return jnp.maximum(x, 0.0) + jnp.log(1.0 + jnp.exp(-jnp.abs(x)))


def _level_masks():
    row = lax.broadcasted_iota(jnp.int32, (CHUNK, CHUNK), 0)
    col = lax.broadcasted_iota(jnp.int32, (CHUNK, CHUNK), 1)
    sels = []
    for lb in range(CHUNK.bit_length() - 1):
        sels.append(((row >> (lb + 1)) == (col >> (lb + 1))) & (((row >> lb) & 1) == 1)
                    & (((col >> lb) & 1) == 0))
    return row, col, sels


def _unit_lower_inverse(a_list, eye, sels):
    ts = [eye - jnp.where(sels[0], a, 0.0) for a in a_list]
    for sel in sels[1:]:
        tb = [t.astype(BF16) for t in ts]
        ps = [_dot(jnp.where(sel, a, 0.0).astype(BF16), t) for a, t in zip(a_list, tb)]
        ts = [t - _dot(t16, p.astype(BF16)) for t, t16, p in zip(ts, tb, ps)]
    return ts


def _gdn_kernel(x_ref, z_ref, abc_ref, alc_ref, dtc_ref, ng_ref, o_ref, st):
    nbatch = x_ref.shape[0]

    @pl.when(pl.program_id(0) == 0)
    def _():
        st[...] = jnp.zeros_like(st)

    row, col, sels = _level_masks()
    tril = col <= row
    strict = col < row
    eye = jnp.where(row == col, 1.0, 0.0).astype(F32)
    lower = jnp.where(tril, 1.0, 0.0).astype(F32)
    upper = jnp.where(row <= col, 1.0, 0.0).astype(F32)
    hi = lax.Precision.HIGHEST

    def group(gi, carry):
        items = []
        for bb in range(GDN_GB):
            b = gi * GDN_GB + bb
            abc = abc_ref[b]
            g_col = -jnp.exp(alc_ref[...]) * _softplus(abc + dtc_ref[...])
            gc_col = jnp.dot(lower, g_col, precision=hi, preferred_element_type=F32)
            gc_row = lax.dot_general(g_col, upper, (((0,), (0,)), ((), ())), precision=hi,
                                     preferred_element_type=F32)
            beta_all = jax.nn.sigmoid(abc)
            for h in range(GDN_HEADS):
                items.append(dict(
                    b=b, h=h,
                    q=x_ref[b, :, h * GDN_DIM:(h + 1) * GDN_DIM],
                    k=x_ref[b, :, GDN_W + h * GDN_DIM:GDN_W + (h + 1) * GDN_DIM],
                    v=x_ref[b, :, 2 * GDN_W + h * GDN_DIM:2 * GDN_W + (h + 1) * GDN_DIM],
                    gcc=gc_col[:, h:h + 1], gcr=gc_row[h:h + 1, :],
                    gl=gc_col[CHUNK - 1:CHUNK, h:h + 1], beta=beta_all[:, 4 + h:5 + h]))

        kqs = []
        for it in items:
            it["kb"] = it["k"] * it["beta"]
            kqs.append(_dot_nt(jnp.concatenate([it["kb"], it["q"]], axis=0).astype(BF16),
                               it["k"].astype(BF16)))
        a_list = []
        for it, kq in zip(items, kqs):
            decay = jnp.exp(jnp.where(tril, it["gcc"] - it["gcr"], NEG))
            a_list.append(jnp.where(strict, kq[:CHUNK] * decay, 0.0))
            it["qk"] = jnp.where(tril, kq[CHUNK:] * decay, 0.0).astype(BF16)
        ts = _unit_lower_inverse(a_list, eye, sels)
        uws = []
        for it, t in zip(items, ts):
            eg = jnp.exp(it["gcc"])
            rhs = jnp.concatenate([it["v"] * it["beta"], it["kb"] * eg], axis=1).astype(BF16)
            uws.append(_dot(t.astype(BF16), rhs))
            it["qd"] = (it["q"] * eg).astype(BF16)
            it["kd"] = (it["k"] * jnp.exp(it["gl"] - it["gcc"])).astype(BF16)

        states, wss = [], []
        for it, uw in zip(items, uws):
            s = st[it["b"] * GDN_HEADS + it["h"]]
            states.append(s)
            lhs = jnp.concatenate([uw[:, GDN_DIM:].astype(BF16), it["qd"]], axis=0)
            wss.append(_dot(lhs, s.astype(BF16)))
        for it, uw, s, ws in zip(items, uws, states, wss):
            vb = (uw[:, :GDN_DIM] - ws[:CHUNK]).astype(BF16)
            o = ws[CHUNK:] + _dot(it["qk"], vb)
            st[it["b"] * GDN_HEADS + it["h"]] = s * jnp.exp(it["gl"]) + _dot_tn(it["kd"], vb)
            hl = slice(it["h"] * GDN_DIM, (it["h"] + 1) * GDN_DIM)
            zz = z_ref[it["b"], :, hl]
            o = o * lax.rsqrt(jnp.mean(o * o, axis=-1, keepdims=True) + EPS)
            o_ref[it["b"], :, hl] = o * ng_ref[...] * (zz * jax.nn.sigmoid(zz))
        return carry

    lax.fori_loop(0, nbatch // GDN_GB, group, 0)


def _gdn(qkva, z, abc, alc, dtc, ng):
    batch, seq, _ = qkva.shape
    blk = lambda w: pl.BlockSpec((batch, CHUNK, w), lambda c: (0, c, 0))
    const = lambda shape: pl.BlockSpec(shape, lambda c: (0, 0))
    return pl.pallas_call(
        _gdn_kernel,
        grid=(seq // CHUNK,),
        in_specs=[blk(3 * GDN_W), blk(GDN_W), blk(128),
                  const((1, 128)), const((1, 128)), const((1, GDN_DIM))],
        out_specs=blk(GDN_W),
        out_shape=jax.ShapeDtypeStruct((batch, seq, GDN_W), F32),
        scratch_shapes=[pltpu.VMEM((batch * GDN_HEADS, GDN_DIM, GDN_DIM), F32)],
        compiler_params=_params("arbitrary"),
        name="gdn",
    )(qkva, z, abc, alc, dtc, ng)


def _dil_bias(delta, dilation, slopes):
    valid = (delta >= 0) & (delta <= SWA_BLOCK)
    df = delta.astype(F32) * LOG2E
    return jnp.concatenate([jnp.where(valid, -(s * float(dilation)) * df, NEG) for s in slopes], axis=0)


def _dil_tables(hp, b1, b4, b16):
    def iota(shape, dim):
        return lax.broadcasted_iota(jnp.int32, shape, dim)

    def slope(head, shape):
        e = jnp.full(shape, 126, jnp.int32) - head
        return lax.bitcast_convert_type(lax.shift_left(e, jnp.full_like(e, 23)), F32)

    wide = (SWA_BLOCK, 2 * SWA_BLOCK)
    sl = [slope(2 * hp, wide), slope(2 * hp + 1, wide)]
    p, c = iota(wide, 0), iota(wide, 1)
    d = 16 * ((p & 7) - (c & 15)) + ((p >> 3) - (c >> 4))
    b1[0] = _dil_bias(d + 16 * 8, 1, sl)
    b1[1] = _dil_bias(d, 1, sl)
    d = 4 * ((p & 31) - (c & 63)) + ((p >> 5) - (c >> 6))
    b4[0] = _dil_bias(d + 4 * 32, 4, sl)
    b4[1] = _dil_bias(d, 4, sl)
    sq = (SWA_BLOCK, SWA_BLOCK)
    b16[...] = _dil_bias(iota(sq, 0) - iota(sq, 1), 16, [slope(2 * hp, sq), slope(2 * hp + 1, sq)])


def _attend(tiles, head_a):
    scores = []
    for q, k, v, bias in tiles:
        qa = jnp.where(head_a, q, 0.0)
        qs = (jnp.concatenate([qa, q - qa], axis=0) * (SWA_DIM ** -0.5 * LOG2E)).astype(BF16)
        scores.append(_dot_nt(qs, k.astype(BF16)) + bias)
    probs = []
    for s in scores:
        m = jnp.max(s, axis=-1, keepdims=True)
        e = jnp.exp2(s - m)
        l = jnp.sum(e, axis=-1, keepdims=True)
        probs.append((e.astype(BF16), 1.0 / l, m + jnp.log2(l)))
    outs = []
    for (q, k, v, bias), (e, inv, lse) in zip(tiles, probs):
        o = _dot(e, v.astype(BF16)) * inv
        outs.append((jnp.where(head_a, o[:SWA_BLOCK], o[SWA_BLOCK:]),
                     jnp.where(head_a, lse[:SWA_BLOCK], lse[SWA_BLOCK:])))
    return outs


def _dil_kernel(q_ref, k_ref, v_ref, o_ref, b1, b4, b16, o1, l1, o4, l4, o16, l16):
    hp = pl.program_id(0)

    @pl.when(pl.program_id(1) == 0)
    def _():
        _dil_tables(hp, b1, b4, b16)

    head_a = lax.broadcasted_iota(jnp.int32, (1, 2 * SWA_DIM), 1) < SWA_DIM
    u = DIL_UNROLL
    nblk = q_ref.shape[2] * RES // SWA_BLOCK

    def cat(ref, pieces):
        return jnp.concatenate([ref[0, r, rows, :] for r, rows in pieces], axis=0)

    def body1(it, carry):
        tiles, where = [], []
        for j in range(u):
            n = it * u + j
            qrows = pl.ds(pl.multiple_of(n * 8, 8), 8)
            krows = pl.ds(pl.multiple_of(jnp.maximum(n * 8 - 8, 0), 8), 16)
            bias = b1[0] if j else jnp.where(n == 0, b1[1], b1[0])
            tiles.append((cat(q_ref, [(r, qrows) for r in range(RES)]),
                          cat(k_ref, [(r, krows) for r in range(RES)]),
                          cat(v_ref, [(r, krows) for r in range(RES)]), bias))
            where.append(qrows)
        for qrows, (o, lse) in zip(where, _attend(tiles, head_a)):
            for r in range(RES):
                o1[r, qrows, :] = o[8 * r:8 * r + 8]
                l1[r, qrows, :] = lse[8 * r:8 * r + 8]
        return carry

    lax.fori_loop(0, nblk // u, body1, 0)

    def body4(it, carry):
        tiles, where = [], []
        for j in range(u):
            r4, n = it * (u // 4) + j // 4, j % 4
            qrows = slice(32 * n, 32 * n + 32)
            k0 = max(32 * n - 32, 0)
            krows = slice(k0, k0 + 64)
            grp = [r4 + 4 * g for g in range(4)]
            tiles.append((cat(q_ref, [(r, qrows) for r in grp]), cat(k_ref, [(r, krows) for r in grp]),
                          cat(v_ref, [(r, krows) for r in grp]), b4[0] if n else b4[1]))
            where.append((grp, qrows))
        for (grp, qrows), (o, lse) in zip(where, _attend(tiles, head_a)):
            for g, r in enumerate(grp):
                o4[r, qrows, :] = o[32 * g:32 * g + 32]
                l4[r, qrows, :] = lse[32 * g:32 * g + 32]
        return carry

    lax.fori_loop(0, nblk // u, body4, 0)

    def body16(it, carry):
        rs = [it * u + j for j in range(u)]
        tiles = [(q_ref[0, r], k_ref[0, r], v_ref[0, r], b16[...]) for r in rs]
        for r, (o, lse) in zip(rs, _attend(tiles, head_a)):
            o16[r] = o
            l16[r] = lse
        return carry

    lax.fori_loop(0, RES // u, body16, 0)

    la, lb, lc = l1[...], l4[...], l16[...]
    m = jnp.maximum(jnp.maximum(la, lb), lc)
    wa, wb, wc = jnp.exp2(la - m), jnp.exp2(lb - m), jnp.exp2(lc - m)
    o_ref[0] = (wa * o1[...] + wb * o4[...] + wc * o16[...]) / (wa + wb + wc)


def _dilated(qkvb):
    batch, _, sub, _ = qkvb.shape
    npair = SWA_HEADS // 2
    spec = lambda off: pl.BlockSpec((1, RES, sub, 2 * SWA_DIM), lambda p, b: (b, 0, 0, off + p))
    wide = (2 * SWA_BLOCK, 2 * SWA_BLOCK)
    res = pltpu.VMEM((RES, sub, 2 * SWA_DIM), F32)
    return pl.pallas_call(
        _dil_kernel,
        grid=(npair, batch),
        in_specs=[spec(0), spec(npair), spec(2 * npair)],
        out_specs=spec(0),
        out_shape=jax.ShapeDtypeStruct((batch, RES, sub, SWA_W), F32),
        scratch_shapes=[pltpu.VMEM((2,) + wide, F32), pltpu.VMEM((2,) + wide, F32),
                        pltpu.VMEM((2 * SWA_BLOCK, SWA_BLOCK), F32), res, res, res, res, res, res],
        compiler_params=_params("arbitrary", "arbitrary"),
        name="dilated_attn",
    )(qkvb, qkvb, qkvb)


def _memkv_kernel(m_ref, g_ref, wk_ref, wv_ref, k_ref, v_ref):
    mn = _rms(m_ref[...], g_ref[...]).astype(BF16)
    k_ref[...] = _dot(mn, wk_ref[...]).astype(BF16)
    v_ref[...] = _dot(mn, wv_ref[...]).astype(BF16)


def _memkv(mem, g, wk, wv):
    t = mem.shape[0]
    tm = min(TM, t)
    row = pl.BlockSpec((tm, D_MODEL), lambda i: (i, 0))
    return pl.pallas_call(
        _memkv_kernel,
        grid=(t // tm,),
        in_specs=[row, _resident((1, D_MODEL)), _resident((D_MODEL, D_MODEL)),
                  _resident((D_MODEL, D_MODEL))],
        out_specs=[row, row],
        out_shape=[jax.ShapeDtypeStruct((t, D_MODEL), BF16)] * 2,
        compiler_params=_params("parallel"),
        name="mem_kv",
    )(mem, g, wk, wv)


def _outmem_kernel(x_ref, oa_ref, ob_ref, woa_ref, wob_ref, gmix_ref, gpre_ref, wq_ref,
                   k_ref, v_ref, wo_ref, gpost_ref, o_ref, att, obn):
    rs = TM // OUT_SUB
    subs = [slice(i * rs, (i + 1) * rs) for i in range(OUT_SUB)]
    mixes = [_dot(oa_ref[s, :].astype(BF16), woa_ref[...]) for s in subs]
    for l in range(SWA_W // 128):
        for r in range(RES):
            obn[l, pl.ds(r, TM // RES, stride=RES), :] = ob_ref[0, r, :, 128 * l:128 * l + 128]
    xs = []
    for s, mix in zip(subs, mixes):
        ob = jnp.concatenate([obn[l, s, :] for l in range(SWA_W // 128)], axis=1).astype(BF16)
        xs.append(x_ref[s, :] + _rms(mix + _dot(ob, wob_ref[...]), gmix_ref[...]))
    qs = [(_dot(_rms(x, gpre_ref[...]).astype(BF16), wq_ref[...]) * (MEM_DIM ** -0.5)).astype(BF16)
          for x in xs]
    for h in range(MEM_HEADS):
        hl = slice(h * MEM_DIM, (h + 1) * MEM_DIM)
        scores = [_dot_nt(q[:, hl], k_ref[:, hl]) for q in qs]
        for s, sc in zip(subs, scores):
            e = jnp.exp(sc - jnp.max(sc, axis=-1, keepdims=True))
            p = e / jnp.sum(e, axis=-1, keepdims=True)
            att[s, hl] = _dot(p.astype(BF16), v_ref[:, hl]).astype(BF16)
    cs = [_dot(att[s, :], wo_ref[...]) for s in subs]
    for s, x, c in zip(subs, xs, cs):
        o_ref[s, :] = x + _rms(c, gpost_ref[...])


def _outmem(x, oa, ob, woa, wob, gmix, gpre, wq, kmem, vmem, wo, gpost, seq, n_mem):
    t = x.shape[0]
    per_seq = seq // TM
    row = lambda w: pl.BlockSpec((TM, w), lambda i: (i, 0))
    kv = pl.BlockSpec((n_mem, D_MODEL), lambda i: (i // per_seq, 0))
    return pl.pallas_call(
        _outmem_kernel,
        grid=(t // TM,),
        in_specs=[row(D_MODEL), row(GDN_W),
                  pl.BlockSpec((1, RES, TM // RES, SWA_W), lambda i: (i // per_seq, 0, i % per_seq, 0)),
                  _resident((GDN_W, D_MODEL)), _resident((SWA_W, D_MODEL)), _resident((1, D_MODEL)),
                  _resident((1, D_MODEL)), _resident((D_MODEL, D_MODEL)), kv, kv,
                  _resident((D_MODEL, D_MODEL)), _resident((1, D_MODEL))],
        out_specs=row(D_MODEL),
        out_shape=jax.ShapeDtypeStruct((t, D_MODEL), F32),
        scratch_shapes=[pltpu.VMEM((TM, D_MODEL), BF16), pltpu.VMEM((SWA_W // 128, TM, 128), F32)],
        compiler_params=_params("parallel"),
        name="out_mem_attn",
    )(x, oa, ob, woa, wob, gmix, gpre, wq, kmem, vmem, wo, gpost)


def _layer(x, mem, p, batch, seq, n_mem):
    bf = lambda w: w.astype(BF16)
    row = lambda g: g.reshape(1, -1)
    x = _ffn(x, row(p["ffn1_pre_g"]), bf(p["ffn1_w_gate"]), bf(p["ffn1_w_up"]),
             bf(p["ffn1_w_down"]), row(p["ffn1_post_g"]))

    w_in = p["w_in"]
    o_z, o_a, o_b = 3 * GDN_W, 4 * GDN_W, 4 * GDN_W + 2 * GDN_HEADS
    wab = jnp.pad(w_in[:, o_a:o_b], ((0, 0), (0, 128 - 2 * GDN_HEADS)))
    cw = jnp.pad(p["gdn_conv_w"], ((0, 8 - GDN_CONV), (0, 0)))
    qkva, z, abc, qkvb = _proj(x, row(p["mix_pre_g"]), bf(w_in[:, :o_z]), bf(w_in[:, o_z:o_a]),
                               bf(wab), bf(w_in[:, o_b:]), cw, batch, seq)

    pad_lane = lambda v: jnp.pad(v.reshape(1, -1), ((0, 0), (0, 128 - GDN_HEADS)))
    per_seq = lambda a: a.reshape(batch, seq, a.shape[-1])
    o_gdn = _gdn(per_seq(qkva), per_seq(z), per_seq(abc), pad_lane(p["gdn_a_log"]),
                 pad_lane(p["gdn_dt_bias"]), row(p["gdn_norm_g"])).reshape(batch * seq, GDN_W)
    o_dil = _dilated(qkvb)

    kmem, vmem = _memkv(mem, row(p["mem_kv_g"]), bf(p["mem_wk"]), bf(p["mem_wv"]))
    w_out = bf(p["w_out"])
    x = _outmem(x, o_gdn, o_dil, w_out[:GDN_W], w_out[GDN_W:], row(p["mix_post_g"]),
                row(p["mem_pre_g"]), bf(p["mem_wq"]), kmem, vmem, bf(p["mem_wo"]),
                row(p["mem_post_g"]), seq, n_mem)

    return _ffn(x, row(p["ffn2_pre_g"]), bf(p["ffn2_w_gate"]), bf(p["ffn2_w_up"]),
                bf(p["ffn2_w_down"]), row(p["ffn2_post_g"]))


_NAMES = ("ffn1_pre_g", "ffn1_w_gate", "ffn1_w_up", "ffn1_w_down", "ffn1_post_g",
          "mix_pre_g", "w_in", "gdn_conv_w", "gdn_a_log", "gdn_dt_bias", "gdn_norm_g", "w_out",
          "mix_post_g", "mem_pre_g", "mem_kv_g", "mem_wq", "mem_wk", "mem_wv", "mem_wo",
          "mem_post_g", "ffn2_pre_g", "ffn2_w_gate", "ffn2_w_up", "ffn2_w_down", "ffn2_post_g")


def kernel(x, mem, ffn1_pre_g, ffn1_w_gate, ffn1_w_up, ffn1_w_down, ffn1_post_g, mix_pre_g, w_in, gdn_conv_w, gdn_a_log, gdn_dt_bias, gdn_norm_g, w_out, mix_post_g, mem_pre_g, mem_kv_g, mem_wq, mem_wk, mem_wv, mem_wo, mem_post_g, ffn2_pre_g, ffn2_w_gate, ffn2_w_up, ffn2_w_down, ffn2_post_g):
    stacked = dict(zip(_NAMES, (ffn1_pre_g, ffn1_w_gate, ffn1_w_up, ffn1_w_down, ffn1_post_g,
                                mix_pre_g, w_in, gdn_conv_w, gdn_a_log, gdn_dt_bias, gdn_norm_g, w_out,
                                mix_post_g, mem_pre_g, mem_kv_g, mem_wq, mem_wk, mem_wv, mem_wo,
                                mem_post_g, ffn2_pre_g, ffn2_w_gate, ffn2_w_up, ffn2_w_down,
                                ffn2_post_g)))
    batch, seq, _ = x.shape
    n_mem = mem.shape[1]
    xf = x.reshape(batch * seq, D_MODEL)
    memf = mem.reshape(batch * n_mem, D_MODEL)
    for l in range(ffn1_pre_g.shape[0]):
        xf = _layer(xf, memf, {k: v[l] for k, v in stacked.items()}, batch, seq, n_mem)
    return xf.reshape(batch, seq, D_MODEL)
```

```python
import functools

import jax
import jax.numpy as jnp
from jax import lax
from jax.experimental import pallas as pl
from jax.experimental.pallas import tpu as pltpu

F32 = jnp.float32
BF16 = jnp.bfloat16

D_MODEL = 1024
D_FF = 2816
EPS = 1e-6
GDN_HEADS = 4
GDN_DIM = 128
GDN_W = GDN_HEADS * GDN_DIM
GDN_CONV = 4
CHUNK = 64
SWA_HEADS = 8
SWA_DIM = 64
SWA_W = SWA_HEADS * SWA_DIM
SWA_BLOCK = 128
DILATIONS = (1, 4, 16)
RES = 16
MEM_HEADS = 4
MEM_DIM = D_MODEL // MEM_HEADS
NEG = -1e30
LOG2E = 1.4426950408889634

V7X_VMEM_BYTES = 64 * 1024 * 1024
VMEM_LIMIT = V7X_VMEM_BYTES - 8 * 1024 * 1024

TM = 512
HALO = 16
FFN_FC = 256
PROJ_TILE = 256
OUT_SUB = 2
GDN_GB = 8
DIL_UNROLL = 8


def _rms(x, g):
    return x * lax.rsqrt(jnp.mean(x * x, axis=-1, keepdims=True) + EPS) * g


def _dot(a, b):
    return jnp.dot(a, b, preferred_element_type=F32)


def _dot_nt(a, b):
    return lax.dot_general(a, b, (((1,), (1,)), ((), ())), preferred_element_type=F32)


def _dot_tn(a, b):
    return lax.dot_general(a, b, (((0,), (0,)), ((), ())), preferred_element_type=F32)


def _resident(shape):
    return pl.BlockSpec(shape, lambda *_: (0,) * len(shape), pipeline_mode=pl.Buffered(1))


def _params(*sem):
    return pltpu.CompilerParams(dimension_semantics=sem, vmem_limit_bytes=VMEM_LIMIT)


def _ffn_kernel(x_ref, gpre_ref, wg_ref, wu_ref, wd_ref, gpost_ref, o_ref, a_ref):
    x = x_ref[...]
    xn = _rms(x, gpre_ref[...]).astype(BF16)
    for c in range(D_FF // FFN_FC):
        sl = slice(c * FFN_FC, (c + 1) * FFN_FC)
        g = _dot(xn, wg_ref[:, sl])
        u = _dot(xn, wu_ref[:, sl])
        a_ref[:, sl] = (g * jax.nn.sigmoid(g) * u).astype(BF16)
    f = _dot(a_ref[...], wd_ref[...])
    o_ref[...] = x + 0.5 * _rms(f, gpost_ref[...])


def _ffn(x, gpre, wg, wu, wd, gpost):
    t = x.shape[0]
    row = pl.BlockSpec((TM, D_MODEL), lambda i: (i, 0))
    return pl.pallas_call(
        _ffn_kernel,
        grid=(t // TM,),
        in_specs=[row, _resident((1, D_MODEL)), _resident((D_MODEL, D_FF)),
                  _resident((D_MODEL, D_FF)), _resident((D_FF, D_MODEL)),
                  _resident((1, D_MODEL))],
        out_specs=row,
        out_shape=jax.ShapeDtypeStruct((t, D_MODEL), F32),
        scratch_shapes=[pltpu.VMEM((TM, D_FF), BF16)],
        compiler_params=_params("parallel"),
        name="ffn",
    )(x, gpre, wg, wu, wd, gpost)


def _proj_kernel(x_ref, xh_ref, g_ref, wa_ref, wz_ref, wab_ref, wb_ref, cw_ref,
                 qkva_ref, z_ref, abc_ref, qkvb_ref, rb, *, per_seq):
    g = g_ref[...]
    h = _rms(x_ref[...], g).astype(BF16)
    keep = jnp.where(pl.program_id(0) % per_seq == 0, 0.0, 1.0)
    hx = jnp.concatenate([(_rms(xh_ref[...], g) * keep).astype(BF16), h], axis=0)

    def conv_tile(t, xa):
        for hd in range(PROJ_TILE // GDN_DIM):
            lanes = slice(PROJ_TILE * t + hd * GDN_DIM, PROJ_TILE * t + (hd + 1) * GDN_DIM)
            xh = xa[:, hd * GDN_DIM:(hd + 1) * GDN_DIM]
            y = cw_ref[0:1, lanes] * xh
            for j in range(1, GDN_CONV):
                y = pltpu.roll(y, 1, axis=0) + cw_ref[j:j + 1, lanes] * xh
            y = y[HALO:]
            y = y * jax.nn.sigmoid(y)
            if lanes.start < 2 * GDN_W:
                scale = GDN_DIM ** -0.5 if lanes.start < GDN_W else 1.0
                y = y * (lax.rsqrt(jnp.sum(y * y, axis=-1, keepdims=True) + EPS) * scale)
            qkva_ref[:, lanes] = y

    def attn_tile(t):
        yb = _dot(h, wb_ref[:, PROJ_TILE * t:PROJ_TILE * (t + 1)])
        for half in range(PROJ_TILE // 128):
            l = t * (PROJ_TILE // 128) + half
            rb[l] = yb[:, 128 * half:128 * half + 128]
            for r in range(RES):
                qkvb_ref[0, r, :, 128 * l:128 * l + 128] = rb[l, pl.ds(r, TM // RES, stride=RES), :]

    n_a = 3 * GDN_W // PROJ_TILE
    prev = None
    for t in range(n_a):
        xa = _dot(hx, wa_ref[:, PROJ_TILE * t:PROJ_TILE * (t + 1)])
        attn_tile(t)
        if t:
            conv_tile(t - 1, prev)
        prev = xa
    z_ref[...] = _dot(h, wz_ref[...])
    abc_ref[...] = _dot(h, wab_ref[...])
    conv_tile(n_a - 1, prev)


def _proj(x, g, wa, wz, wab, wb, cw, batch, seq):
    t = x.shape[0]
    per_seq = seq // TM
    row = lambda w: pl.BlockSpec((TM, w), lambda i: (i, 0))
    halo = pl.BlockSpec((HALO, D_MODEL), lambda i: (jnp.maximum(i * (TM // HALO) - 1, 0), 0))
    return pl.pallas_call(
        functools.partial(_proj_kernel, per_seq=per_seq),
        grid=(t // TM,),
        in_specs=[row(D_MODEL), halo, _resident((1, D_MODEL)), _resident(wa.shape),
                  _resident(wz.shape), _resident(wab.shape), _resident(wb.shape),
                  _resident(cw.shape)],
        out_specs=[row(3 * GDN_W), row(GDN_W), row(128),
                   pl.BlockSpec((1, RES, TM // RES, 3 * SWA_W),
                                lambda i: (i // per_seq, 0, i % per_seq, 0))],
        out_shape=[jax.ShapeDtypeStruct((t, 3 * GDN_W), F32),
                   jax.ShapeDtypeStruct((t, GDN_W), F32),
                   jax.ShapeDtypeStruct((t, 128), F32),
                   jax.ShapeDtypeStruct((batch, RES, seq // RES, 3 * SWA_W), F32)],
        scratch_shapes=[pltpu.VMEM((3 * SWA_W // 128, TM, 128), F32)],
        compiler_params=_params("parallel"),
        name="mixer_proj",
    )(x, x, g, wa, wz, wab, wb, cw)


def _softplus(x):
    return jnp.maximum(x, 0.0) + jnp.log(1.0 + jnp.exp(-jnp.abs(x)))


def _level_masks():
    row = lax.broadcasted_iota(jnp.int32, (CHUNK, CHUNK), 0)
    col = lax.broadcasted_iota(jnp.int32, (CHUNK, CHUNK), 1)
    sels = []
    for lb in range(CHUNK.bit_length() - 1):
        sels.append(((row >> (lb + 1)) == (col >> (lb + 1))) & (((row >> lb) & 1) == 1)
                    & (((col >> lb) & 1) == 0))
    return row, col, sels


def _unit_lower_inverse(a_list, eye, sels):
    ts = [eye - jnp.where(sels[0], a, 0.0) for a in a_list]
    for sel in sels[1:]:
        tb = [t.astype(BF16) for t in ts]
        ps = [_dot(jnp.where(sel, a, 0.0).astype(BF16), t) for a, t in zip(a_list, tb)]
        ts = [t - _dot(t16, p.astype(BF16)) for t, t16, p in zip(ts, tb, ps)]
    return ts


def _gdn_kernel(x_ref, z_ref, abc_ref, alc_ref, dtc_ref, ng_ref, o_ref, st):
    nbatch = x_ref.shape[0]

    @pl.when(pl.program_id(0) == 0)
    def _():
        st[...] = jnp.zeros_like(st)

    row, col, sels = _level_masks()
    tril = col <= row
    strict = col < row
    eye = jnp.where(row == col, 1.0, 0.0).astype(F32)
    lower = jnp.where(tril, 1.0, 0.0).astype(F32)
    upper = jnp.where(row <= col, 1.0, 0.0).astype(F32)
    hi = lax.Precision.HIGHEST

    def group(gi, carry):
        items = []
        for bb in range(GDN_GB):
            b = gi * GDN_GB + bb
            abc = abc_ref[b]
            g_col = -jnp.exp(alc_ref[...]) * _softplus(abc + dtc_ref[...])
            gc_col = jnp.dot(lower, g_col, precision=hi, preferred_element_type=F32)
            gc_row = lax.dot_general(g_col, upper, (((0,), (0,)), ((), ())), precision=hi,
                                     preferred_element_type=F32)
            beta_all = jax.nn.sigmoid(abc)
            for h in range(GDN_HEADS):
                items.append(dict(
                    b=b, h=h,
                    q=x_ref[b, :, h * GDN_DIM:(h + 1) * GDN_DIM],
                    k=x_ref[b, :, GDN_W + h * GDN_DIM:GDN_W + (h + 1) * GDN_DIM],
                    v=x_ref[b, :, 2 * GDN_W + h * GDN_DIM:2 * GDN_W + (h + 1) * GDN_DIM],
                    gcc=gc_col[:, h:h + 1], gcr=gc_row[h:h + 1, :],
                    gl=gc_col[CHUNK - 1:CHUNK, h:h + 1], beta=beta_all[:, 4 + h:5 + h]))

        kqs = []
        for it in items:
            it["kb"] = it["k"] * it["beta"]
            kqs.append(_dot_nt(jnp.concatenate([it["kb"], it["q"]], axis=0).astype(BF16),
                               it["k"].astype(BF16)))
        a_list = []
        for it, kq in zip(items, kqs):
            decay = jnp.exp(jnp.where(tril, it["gcc"] - it["gcr"], NEG))
            a_list.append(jnp.where(strict, kq[:CHUNK] * decay, 0.0))
            it["qk"] = jnp.where(tril, kq[CHUNK:] * decay, 0.0).astype(BF16)
        ts = _unit_lower_inverse(a_list, eye, sels)
        uws = []
        for it, t in zip(items, ts):
            eg = jnp.exp(it["gcc"])
            rhs = jnp.concatenate([it["v"] * it["beta"], it["kb"] * eg], axis=1).astype(BF16)
            uws.append(_dot(t.astype(BF16), rhs))
            it["qd"] = (it["q"] * eg).astype(BF16)
            it["kd"] = (it["k"] * jnp.exp(it["gl"] - it["gcc"])).astype(BF16)

        states, wss = [], []
        for it, uw in zip(items, uws):
            s = st[it["b"] * GDN_HEADS + it["h"]]
            states.append(s)
            lhs = jnp.concatenate([uw[:, GDN_DIM:].astype(BF16), it["qd"]], axis=0)
            wss.append(_dot(lhs, s.astype(BF16)))
        for it, uw, s, ws in zip(items, uws, states, wss):
            vb = (uw[:, :GDN_DIM] - ws[:CHUNK]).astype(BF16)
            o = ws[CHUNK:] + _dot(it["qk"], vb)
            st[it["b"] * GDN_HEADS + it["h"]] = s * jnp.exp(it["gl"]) + _dot_tn(it["kd"], vb)
            hl = slice(it["h"] * GDN_DIM, (it["h"] + 1) * GDN_DIM)
            zz = z_ref[it["b"], :, hl]
            o = o * lax.rsqrt(jnp.mean(o * o, axis=-1, keepdims=True) + EPS)
            o_ref[it["b"], :, hl] = o * ng_ref[...] * (zz * jax.nn.sigmoid(zz))
        return carry

    lax.fori_loop(0, nbatch // GDN_GB, group, 0)


def _gdn(qkva, z, abc, alc, dtc, ng):
    batch, seq, _ = qkva.shape
    blk = lambda w: pl.BlockSpec((batch, CHUNK, w), lambda c: (0, c, 0))
    const = lambda shape: pl.BlockSpec(shape, lambda c: (0, 0))
    return pl.pallas_call(
        _gdn_kernel,
        grid=(seq // CHUNK,),
        in_specs=[blk(3 * GDN_W), blk(GDN_W), blk(128),
                  const((1, 128)), const((1, 128)), const((1, GDN_DIM))],
        out_specs=blk(GDN_W),
        out_shape=jax.ShapeDtypeStruct((batch, seq, GDN_W), F32),
        scratch_shapes=[pltpu.VMEM((batch * GDN_HEADS, GDN_DIM, GDN_DIM), F32)],
        compiler_params=_params("arbitrary"),
        name="gdn",
    )(qkva, z, abc, alc, dtc, ng)


def _dil_bias(delta, dilation, slopes):
    valid = (delta >= 0) & (delta <= SWA_BLOCK)
    df = delta.astype(F32) * LOG2E
    return jnp.concatenate([jnp.where(valid, -(s * float(dilation)) * df, NEG) for s in slopes], axis=0)


def _dil_tables(hp, b1, b4, b16):
    def iota(shape, dim):
        return lax.broadcasted_iota(jnp.int32, shape, dim)

    def slope(head, shape):
        e = jnp.full(shape, 126, jnp.int32) - head
        return lax.bitcast_convert_type(lax.shift_left(e, jnp.full_like(e, 23)), F32)

    wide = (SWA_BLOCK, 2 * SWA_BLOCK)
    sl = [slope(2 * hp, wide), slope(2 * hp + 1, wide)]
    p, c = iota(wide, 0), iota(wide, 1)
    d = 16 * ((p & 7) - (c & 15)) + ((p >> 3) - (c >> 4))
    b1[0] = _dil_bias(d + 16 * 8, 1, sl)
    b1[1] = _dil_bias(d, 1, sl)
    d = 4 * ((p & 31) - (c & 63)) + ((p >> 5) - (c >> 6))
    b4[0] = _dil_bias(d + 4 * 32, 4, sl)
    b4[1] = _dil_bias(d, 4, sl)
    sq = (SWA_BLOCK, SWA_BLOCK)
    b16[...] = _dil_bias(iota(sq, 0) - iota(sq, 1), 16, [slope(2 * hp, sq), slope(2 * hp + 1, sq)])


def _attend(tiles, head_a):
    scores = []
    for q, k, v, bias in tiles:
        qa = jnp.where(head_a, q, 0.0)
        qs = (jnp.concatenate([qa, q - qa], axis=0) * (SWA_DIM ** -0.5 * LOG2E)).astype(BF16)
        scores.append(_dot_nt(qs, k.astype(BF16)) + bias)
    probs = []
    for s in scores:
        m = jnp.max(s, axis=-1, keepdims=True)
        probs.append((jnp.exp2(s - m).astype(BF16), m))
    outs = []
    for (q, k, v, bias), (e, m) in zip(tiles, probs):
        v1 = jnp.concatenate([v.astype(BF16), jnp.ones(v.shape, BF16)], axis=1)
        ol = _dot(e, v1)
        top, bot = ol[:SWA_BLOCK], ol[SWA_BLOCK:]
        w = 2 * SWA_DIM
        outs.append((jnp.where(head_a, top[:, :w], bot[:, :w]), jnp.where(head_a, top[:, w:], bot[:, w:]),
                     jnp.where(head_a, m[:SWA_BLOCK], m[SWA_BLOCK:])))
    return outs


def _dil_kernel(q_ref, k_ref, v_ref, o_ref, b1, b4, b16, s1, s4, s16):
    hp = pl.program_id(0)

    @pl.when(pl.program_id(1) == 0)
    def _():
        _dil_tables(hp, b1, b4, b16)

    head_a = lax.broadcasted_iota(jnp.int32, (1, 2 * SWA_DIM), 1) < SWA_DIM
    u = DIL_UNROLL
    nblk = q_ref.shape[2] * RES // SWA_BLOCK

    def cat(ref, pieces):
        return jnp.concatenate([ref[0, r, rows, :] for r, rows in pieces], axis=0)

    def body1(it, carry):
        tiles, where = [], []
        for j in range(u):
            n = it * u + j
            qrows = pl.ds(pl.multiple_of(n * 8, 8), 8)
            krows = pl.ds(pl.multiple_of(jnp.maximum(n * 8 - 8, 0), 8), 16)
            bias = b1[0] if j else jnp.where(n == 0, b1[1], b1[0])
            tiles.append((cat(q_ref, [(r, qrows) for r in range(RES)]),
                          cat(k_ref, [(r, krows) for r in range(RES)]),
                          cat(v_ref, [(r, krows) for r in range(RES)]), bias))
            where.append(qrows)
        for qrows, stats in zip(where, _attend(tiles, head_a)):
            for i, val in enumerate(stats):
                for r in range(RES):
                    s1[i, r, qrows, :] = val[8 * r:8 * r + 8]
        return carry

    lax.fori_loop(0, nblk // u, body1, 0)

    def body4(it, carry):
        tiles, where = [], []
        for j in range(u):
            r4, n = it * (u // 4) + j // 4, j % 4
            qrows = slice(32 * n, 32 * n + 32)
            k0 = max(32 * n - 32, 0)
            krows = slice(k0, k0 + 64)
            grp = [r4 + 4 * g for g in range(4)]
            tiles.append((cat(q_ref, [(r, qrows) for r in grp]), cat(k_ref, [(r, krows) for r in grp]),
                          cat(v_ref, [(r, krows) for r in grp]), b4[0] if n else b4[1]))
            where.append((grp, qrows))
        for (grp, qrows), stats in zip(where, _attend(tiles, head_a)):
            for i, val in enumerate(stats):
                for g, r in enumerate(grp):
                    s4[i, r, qrows, :] = val[32 * g:32 * g + 32]
        return carry

    lax.fori_loop(0, nblk // u, body4, 0)

    def body16(it, carry):
        rs = [it * u + j for j in range(u)]
        tiles = [(q_ref[0, r], k_ref[0, r], v_ref[0, r], b16[...]) for r in rs]
        for r, stats in zip(rs, _attend(tiles, head_a)):
            for i, val in enumerate(stats):
                s16[i, r] = val
        return carry

    lax.fori_loop(0, RES // u, body16, 0)

    m = jnp.maximum(jnp.maximum(s1[2], s4[2]), s16[2])
    ws = [jnp.exp2(s[2] - m) for s in (s1, s4, s16)]
    num = sum(w * s[0] for w, s in zip(ws, (s1, s4, s16)))
    den = sum(w * s[1] for w, s in zip(ws, (s1, s4, s16)))
    o_ref[0] = num / den


def _dilated(qkvb):
    batch, _, sub, _ = qkvb.shape
    npair = SWA_HEADS // 2
    spec = lambda off: pl.BlockSpec((1, RES, sub, 2 * SWA_DIM), lambda p, b: (b, 0, 0, off + p))
    wide = (2 * SWA_BLOCK, 2 * SWA_BLOCK)
    res = pltpu.VMEM((3, RES, sub, 2 * SWA_DIM), F32)
    return pl.pallas_call(
        _dil_kernel,
        grid=(npair, batch),
        in_specs=[spec(0), spec(npair), spec(2 * npair)],
        out_specs=spec(0),
        out_shape=jax.ShapeDtypeStruct((batch, RES, sub, SWA_W), F32),
        scratch_shapes=[pltpu.VMEM((2,) + wide, F32), pltpu.VMEM((2,) + wide, F32),
                        pltpu.VMEM((2 * SWA_BLOCK, SWA_BLOCK), F32), res, res, res],
        compiler_params=_params("arbitrary", "arbitrary"),
        name="dilated_attn",
    )(qkvb, qkvb, qkvb)


def _memkv_kernel(m_ref, g_ref, wk_ref, wv_ref, k_ref, v_ref):
    mn = _rms(m_ref[...], g_ref[...]).astype(BF16)
    k_ref[...] = _dot(mn, wk_ref[...]).astype(BF16)
    v_ref[...] = _dot(mn, wv_ref[...]).astype(BF16)


def _memkv(mem, g, wk, wv):
    t = mem.shape[0]
    tm = min(TM, t)
    row = pl.BlockSpec((tm, D_MODEL), lambda i: (i, 0))
    return pl.pallas_call(
        _memkv_kernel,
        grid=(t // tm,),
        in_specs=[row, _resident((1, D_MODEL)), _resident((D_MODEL, D_MODEL)),
                  _resident((D_MODEL, D_MODEL))],
        out_specs=[row, row],
        out_shape=[jax.ShapeDtypeStruct((t, D_MODEL), BF16)] * 2,
        compiler_params=_params("parallel"),
        name="mem_kv",
    )(mem, g, wk, wv)


def _outmem_kernel(x_ref, oa_ref, ob_ref, woa_ref, wob_ref, gmix_ref, gpre_ref, wq_ref,
                   k_ref, v_ref, wo_ref, gpost_ref, o_ref, att, obn):
    rs = TM // OUT_SUB
    subs = [slice(i * rs, (i + 1) * rs) for i in range(OUT_SUB)]
    mixes = [_dot(oa_ref[s, :].astype(BF16), woa_ref[...]) for s in subs]
    for l in range(SWA_W // 128):
        for r in range(RES):
            obn[l, pl.ds(r, TM // RES, stride=RES), :] = ob_ref[0, r, :, 128 * l:128 * l + 128]
    xs = []
    for s, mix in zip(subs, mixes):
        ob = jnp.concatenate([obn[l, s, :] for l in range(SWA_W // 128)], axis=1).astype(BF16)
        xs.append(x_ref[s, :] + _rms(mix + _dot(ob, wob_ref[...]), gmix_ref[...]))
    qs = [(_dot(_rms(x, gpre_ref[...]).astype(BF16), wq_ref[...]) * (MEM_DIM ** -0.5)).astype(BF16)
          for x in xs]
    for h in range(MEM_HEADS):
        hl = slice(h * MEM_DIM, (h + 1) * MEM_DIM)
        scores = [_dot_nt(q[:, hl], k_ref[:, hl]) for q in qs]
        for s, sc in zip(subs, scores):
            e = jnp.exp(sc - jnp.max(sc, axis=-1, keepdims=True))
            p = e / jnp.sum(e, axis=-1, keepdims=True)
            att[s, hl] = _dot(p.astype(BF16), v_ref[:, hl]).astype(BF16)
    cs = [_dot(att[s, :], wo_ref[...]) for s in subs]
    for s, x, c in zip(subs, xs, cs):
        o_ref[s, :] = x + _rms(c, gpost_ref[...])


def _outmem(x, oa, ob, woa, wob, gmix, gpre, wq, kmem, vmem, wo, gpost, seq, n_mem):
    t = x.shape[0]
    per_seq = seq // TM
    row = lambda w: pl.BlockSpec((TM, w), lambda i: (i, 0))
    kv = pl.BlockSpec((n_mem, D_MODEL), lambda i: (i // per_seq, 0))
    return pl.pallas_call(
        _outmem_kernel,
        grid=(t // TM,),
        in_specs=[row(D_MODEL), row(GDN_W),
                  pl.BlockSpec((1, RES, TM // RES, SWA_W), lambda i: (i // per_seq, 0, i % per_seq, 0)),
                  _resident((GDN_W, D_MODEL)), _resident((SWA_W, D_MODEL)), _resident((1, D_MODEL)),
                  _resident((1, D_MODEL)), _resident((D_MODEL, D_MODEL)), kv, kv,
                  _resident((D_MODEL, D_MODEL)), _resident((1, D_MODEL))],
        out_specs=row(D_MODEL),
        out_shape=jax.ShapeDtypeStruct((t, D_MODEL), F32),
        scratch_shapes=[pltpu.VMEM((TM, D_MODEL), BF16), pltpu.VMEM((SWA_W // 128, TM, 128), F32)],
        compiler_params=_params("parallel"),
        name="out_mem_attn",
    )(x, oa, ob, woa, wob, gmix, gpre, wq, kmem, vmem, wo, gpost)


def _layer(x, mem, p, batch, seq, n_mem):
    bf = lambda w: w.astype(BF16)
    row = lambda g: g.reshape(1, -1)
    x = _ffn(x, row(p["ffn1_pre_g"]), bf(p["ffn1_w_gate"]), bf(p["ffn1_w_up"]),
             bf(p["ffn1_w_down"]), row(p["ffn1_post_g"]))

    w_in = p["w_in"]
    o_z, o_a, o_b = 3 * GDN_W, 4 * GDN_W, 4 * GDN_W + 2 * GDN_HEADS
    wab = jnp.pad(w_in[:, o_a:o_b], ((0, 0), (0, 128 - 2 * GDN_HEADS)))
    cw = jnp.pad(p["gdn_conv_w"], ((0, 8 - GDN_CONV), (0, 0)))
    qkva, z, abc, qkvb = _proj(x, row(p["mix_pre_g"]), bf(w_in[:, :o_z]), bf(w_in[:, o_z:o_a]),
                               bf(wab), bf(w_in[:, o_b:]), cw, batch, seq)

    pad_lane = lambda v: jnp.pad(v.reshape(1, -1), ((0, 0), (0, 128 - GDN_HEADS)))
    per_seq = lambda a: a.reshape(batch, seq, a.shape[-1])
    o_gdn = _gdn(per_seq(qkva), per_seq(z), per_seq(abc), pad_lane(p["gdn_a_log"]),
                 pad_lane(p["gdn_dt_bias"]), row(p["gdn_norm_g"])).reshape(batch * seq, GDN_W)
    o_dil = _dilated(qkvb)

    kmem, vmem = _memkv(mem, row(p["mem_kv_g"]), bf(p["mem_wk"]), bf(p["mem_wv"]))
    w_out = bf(p["w_out"])
    x = _outmem(x, o_gdn, o_dil, w_out[:GDN_W], w_out[GDN_W:], row(p["mix_post_g"]),
                row(p["mem_pre_g"]), bf(p["mem_wq"]), kmem, vmem, bf(p["mem_wo"]),
                row(p["mem_post_g"]), seq, n_mem)

    return _ffn(x, row(p["ffn2_pre_g"]), bf(p["ffn2_w_gate"]), bf(p["ffn2_w_up"]),
                bf(p["ffn2_w_down"]), row(p["ffn2_post_g"]))


_NAMES = ("ffn1_pre_g", "ffn1_w_gate", "ffn1_w_up", "ffn1_w_down", "ffn1_post_g",
          "mix_pre_g", "w_in", "gdn_conv_w", "gdn_a_log", "gdn_dt_bias", "gdn_norm_g", "w_out",
          "mix_post_g", "mem_pre_g", "mem_kv_g", "mem_wq", "mem_wk", "mem_wv", "mem_wo",
          "mem_post_g", "ffn2_pre_g", "ffn2_w_gate", "ffn2_w_up", "ffn2_w_down", "ffn2_post_g")


def kernel(x, mem, ffn1_pre_g, ffn1_w_gate, ffn1_w_up, ffn1_w_down, ffn1_post_g, mix_pre_g, w_in, gdn_conv_w, gdn_a_log, gdn_dt_bias, gdn_norm_g, w_out, mix_post_g, mem_pre_g, mem_kv_g, mem_wq, mem_wk, mem_wv, mem_wo, mem_post_g, ffn2_pre_g, ffn2_w_gate, ffn2_w_up, ffn2_w_down, ffn2_post_g):
    stacked = dict(zip(_NAMES, (ffn1_pre_g, ffn1_w_gate, ffn1_w_up, ffn1_w_down, ffn1_post_g,
                                mix_pre_g, w_in, gdn_conv_w, gdn_a_log, gdn_dt_bias, gdn_norm_g, w_out,
                                mix_post_g, mem_pre_g, mem_kv_g, mem_wq, mem_wk, mem_wv, mem_wo,
                                mem_post_g, ffn2_pre_g, ffn2_w_gate, ffn2_w_up, ffn2_w_down,
                                ffn2_post_g)))
    batch, seq, _ = x.shape
    n_mem = mem.shape[1]
    xf = x.reshape(batch * seq, D_MODEL)
    memf = mem.reshape(batch * n_mem, D_MODEL)
    for l in range(ffn1_pre_g.shape[0]):
        xf = _layer(xf, memf, {k: v[l] for k, v in stacked.items()}, batch, seq, n_mem)
    return xf.reshape(batch, seq, D_MODEL)
```

```python
import functools

import jax
import jax.numpy as jnp
from jax import lax
from jax.experimental import pallas as pl
from jax.experimental.pallas import tpu as pltpu

F32 = jnp.float32
BF16 = jnp.bfloat16

D_MODEL = 1024
D_FF = 2816
EPS = 1e-6
GDN_HEADS = 4
GDN_DIM = 128
GDN_W = GDN_HEADS * GDN_DIM
GDN_CONV = 4
CHUNK = 64
SWA_HEADS = 8
SWA_DIM = 64
SWA_W = SWA_HEADS * SWA_DIM
SWA_BLOCK = 128
DILATIONS = (1, 4, 16)
RES = 16
MEM_HEADS = 4
MEM_DIM = D_MODEL // MEM_HEADS
NEG = -1e30
LOG2E = 1.4426950408889634

V7X_VMEM_BYTES = 64 * 1024 * 1024
VMEM_LIMIT = V7X_VMEM_BYTES - 8 * 1024 * 1024

TM = 512
HALO = 16
FFN_FC = 256
PROJ_TILE = 256
TM_OUT = 1024
OUT_SUB = 4
GDN_GB = 8
DIL_UNROLL = 8


def _rms(x, g):
    return x * lax.rsqrt(jnp.mean(x * x, axis=-1, keepdims=True) + EPS) * g


def _dot(a, b):
    return jnp.dot(a, b, preferred_element_type=F32)


def _dot_nt(a, b):
    return lax.dot_general(a, b, (((1,), (1,)), ((), ())), preferred_element_type=F32)


def _dot_tn(a, b):
    return lax.dot_general(a, b, (((0,), (0,)), ((), ())), preferred_element_type=F32)


def _resident(shape):
    return pl.BlockSpec(shape, lambda *_: (0,) * len(shape), pipeline_mode=pl.Buffered(1))


def _params(*sem):
    return pltpu.CompilerParams(dimension_semantics=sem, vmem_limit_bytes=VMEM_LIMIT)


def _ffn_kernel(x_ref, gpre_ref, wg_ref, wu_ref, wd_ref, gpost_ref, o_ref, a_ref):
    x = x_ref[...]
    xn = _rms(x, gpre_ref[...]).astype(BF16)
    for c in range(D_FF // FFN_FC):
        sl = slice(c * FFN_FC, (c + 1) * FFN_FC)
        g = _dot(xn, wg_ref[:, sl])
        u = _dot(xn, wu_ref[:, sl])
        a_ref[:, sl] = (g * jax.nn.sigmoid(g) * u).astype(BF16)
    f = _dot(a_ref[...], wd_ref[...])
    o_ref[...] = x + 0.5 * _rms(f, gpost_ref[...])


def _ffn(x, gpre, wg, wu, wd, gpost):
    t = x.shape[0]
    row = pl.BlockSpec((TM, D_MODEL), lambda i: (i, 0))
    return pl.pallas_call(
        _ffn_kernel,
        grid=(t // TM,),
        in_specs=[row, _resident((1, D_MODEL)), _resident((D_MODEL, D_FF)),
                  _resident((D_MODEL, D_FF)), _resident((D_FF, D_MODEL)),
                  _resident((1, D_MODEL))],
        out_specs=row,
        out_shape=jax.ShapeDtypeStruct((t, D_MODEL), F32),
        scratch_shapes=[pltpu.VMEM((TM, D_FF), BF16)],
        compiler_params=_params("parallel"),
        name="ffn",
    )(x, gpre, wg, wu, wd, gpost)


def _proj_kernel(x_ref, xh_ref, g_ref, wa_ref, wz_ref, wab_ref, wb_ref, cw_ref,
                 qkva_ref, z_ref, abc_ref, qkvb_ref, rb, *, per_seq):
    g = g_ref[...]
    h = _rms(x_ref[...], g).astype(BF16)
    keep = jnp.where(pl.program_id(0) % per_seq == 0, 0.0, 1.0)
    hx = jnp.concatenate([(_rms(xh_ref[...], g) * keep).astype(BF16), h], axis=0)

    def conv_tile(t, xa):
        for hd in range(PROJ_TILE // GDN_DIM):
            lanes = slice(PROJ_TILE * t + hd * GDN_DIM, PROJ_TILE * t + (hd + 1) * GDN_DIM)
            xh = xa[:, hd * GDN_DIM:(hd + 1) * GDN_DIM]
            y = cw_ref[0:1, lanes] * xh
            for j in range(1, GDN_CONV):
                y = pltpu.roll(y, 1, axis=0) + cw_ref[j:j + 1, lanes] * xh
            y = y[HALO:]
            y = y * jax.nn.sigmoid(y)
            if lanes.start < 2 * GDN_W:
                scale = GDN_DIM ** -0.5 if lanes.start < GDN_W else 1.0
                y = y * (lax.rsqrt(jnp.sum(y * y, axis=-1, keepdims=True) + EPS) * scale)
            qkva_ref[:, lanes] = y

    def attn_tile(t):
        yb = _dot(h, wb_ref[:, PROJ_TILE * t:PROJ_TILE * (t + 1)])
        for half in range(PROJ_TILE // 128):
            l = t * (PROJ_TILE // 128) + half
            rb[l] = yb[:, 128 * half:128 * half + 128]
            for r in range(RES):
                qkvb_ref[0, r, :, 128 * l:128 * l + 128] = rb[l, pl.ds(r, TM // RES, stride=RES), :]

    n_a = 3 * GDN_W // PROJ_TILE
    prev = None
    for t in range(n_a):
        xa = _dot(hx, wa_ref[:, PROJ_TILE * t:PROJ_TILE * (t + 1)])
        attn_tile(t)
        if t:
            conv_tile(t - 1, prev)
        prev = xa
    z_ref[...] = _dot(h, wz_ref[...])
    abc_ref[...] = _dot(h, wab_ref[...])
    conv_tile(n_a - 1, prev)


def _proj(x, g, wa, wz, wab, wb, cw, batch, seq):
    t = x.shape[0]
    per_seq = seq // TM
    row = lambda w: pl.BlockSpec((TM, w), lambda i: (i, 0))
    halo = pl.BlockSpec((HALO, D_MODEL), lambda i: (jnp.maximum(i * (TM // HALO) - 1, 0), 0))
    return pl.pallas_call(
        functools.partial(_proj_kernel, per_seq=per_seq),
        grid=(t // TM,),
        in_specs=[row(D_MODEL), halo, _resident((1, D_MODEL)), _resident(wa.shape),
                  _resident(wz.shape), _resident(wab.shape), _resident(wb.shape),
                  _resident(cw.shape)],
        out_specs=[row(3 * GDN_W), row(GDN_W), row(128),
                   pl.BlockSpec((1, RES, TM // RES, 3 * SWA_W),
                                lambda i: (i // per_seq, 0, i % per_seq, 0))],
        out_shape=[jax.ShapeDtypeStruct((t, 3 * GDN_W), F32),
                   jax.ShapeDtypeStruct((t, GDN_W), F32),
                   jax.ShapeDtypeStruct((t, 128), F32),
                   jax.ShapeDtypeStruct((batch, RES, seq // RES, 3 * SWA_W), F32)],
        scratch_shapes=[pltpu.VMEM((3 * SWA_W // 128, TM, 128), F32)],
        compiler_params=_params("parallel"),
        name="mixer_proj",
    )(x, x, g, wa, wz, wab, wb, cw)


def _softplus(x):
    return jnp.maximum(x, 0.0) + jnp.log(1.0 + jnp.exp(-jnp.abs(x)))


def _level_masks():
    row = lax.broadcasted_iota(jnp.int32, (CHUNK, CHUNK), 0)
    col = lax.broadcasted_iota(jnp.int32, (CHUNK, CHUNK), 1)
    sels = []
    for lb in range(CHUNK.bit_length() - 1):
        sels.append(((row >> (lb + 1)) == (col >> (lb + 1))) & (((row >> lb) & 1) == 1)
                    & (((col >> lb) & 1) == 0))
    return row, col, sels


def _unit_lower_inverse(a_list, eye, sels):
    ts = [eye - jnp.where(sels[0], a, 0.0) for a in a_list]
    for sel in sels[1:]:
        tb = [t.astype(BF16) for t in ts]
        ps = [_dot(jnp.where(sel, a, 0.0).astype(BF16), t) for a, t in zip(a_list, tb)]
        ts = [t - _dot(t16, p.astype(BF16)) for t, t16, p in zip(ts, tb, ps)]
    return ts


def _gdn_kernel(x_ref, z_ref, abc_ref, alc_ref, dtc_ref, ng_ref, o_ref, st):
    nbatch = x_ref.shape[0]

    @pl.when(pl.program_id(0) == 0)
    def _():
        st[...] = jnp.zeros_like(st)

    row, col, sels = _level_masks()
    tril = col <= row
    strict = col < row
    eye = jnp.where(row == col, 1.0, 0.0).astype(F32)
    lower = jnp.where(tril, 1.0, 0.0).astype(F32)
    upper = jnp.where(row <= col, 1.0, 0.0).astype(F32)
    hi = lax.Precision.HIGHEST

    def group(gi, carry):
        items = []
        for bb in range(GDN_GB):
            b = gi * GDN_GB + bb
            abc = abc_ref[b]
            g_col = -jnp.exp(alc_ref[...]) * _softplus(abc + dtc_ref[...])
            gc_col = jnp.dot(lower, g_col, precision=hi, preferred_element_type=F32)
            gc_row = lax.dot_general(g_col, upper, (((0,), (0,)), ((), ())), precision=hi,
                                     preferred_element_type=F32)
            beta_all = jax.nn.sigmoid(abc)
            for h in range(GDN_HEADS):
                items.append(dict(
                    b=b, h=h,
                    q=x_ref[b, :, h * GDN_DIM:(h + 1) * GDN_DIM],
                    k=x_ref[b, :, GDN_W + h * GDN_DIM:GDN_W + (h + 1) * GDN_DIM],
                    v=x_ref[b, :, 2 * GDN_W + h * GDN_DIM:2 * GDN_W + (h + 1) * GDN_DIM],
                    gcc=gc_col[:, h:h + 1], gcr=gc_row[h:h + 1, :],
                    gl=gc_col[CHUNK - 1:CHUNK, h:h + 1], beta=beta_all[:, 4 + h:5 + h]))

        kqs = []
        for it in items:
            it["kb"] = it["k"] * it["beta"]
            kqs.append(_dot_nt(jnp.concatenate([it["kb"], it["q"]], axis=0).astype(BF16),
                               it["k"].astype(BF16)))
        a_list = []
        for it, kq in zip(items, kqs):
            decay = jnp.exp(jnp.where(tril, it["gcc"] - it["gcr"], NEG))
            a_list.append(jnp.where(strict, kq[:CHUNK] * decay, 0.0))
            it["qk"] = jnp.where(tril, kq[CHUNK:] * decay, 0.0).astype(BF16)
        ts = _unit_lower_inverse(a_list, eye, sels)
        uws = []
        for it, t in zip(items, ts):
            eg = jnp.exp(it["gcc"])
            rhs = jnp.concatenate([it["v"] * it["beta"], it["kb"] * eg], axis=1).astype(BF16)
            uws.append(_dot(t.astype(BF16), rhs))
            it["qd"] = (it["q"] * eg).astype(BF16)
            it["kd"] = (it["k"] * jnp.exp(it["gl"] - it["gcc"])).astype(BF16)

        states, wss = [], []
        for it, uw in zip(items, uws):
            s = st[it["b"] * GDN_HEADS + it["h"]]
            states.append(s)
            lhs = jnp.concatenate([uw[:, GDN_DIM:].astype(BF16), it["qd"]], axis=0)
            wss.append(_dot(lhs, s.astype(BF16)))
        for it, uw, s, ws in zip(items, uws, states, wss):
            vb = (uw[:, :GDN_DIM] - ws[:CHUNK]).astype(BF16)
            o = ws[CHUNK:] + _dot(it["qk"], vb)
            st[it["b"] * GDN_HEADS + it["h"]] = s * jnp.exp(it["gl"]) + _dot_tn(it["kd"], vb)
            hl = slice(it["h"] * GDN_DIM, (it["h"] + 1) * GDN_DIM)
            zz = z_ref[it["b"], :, hl]
            o = o * lax.rsqrt(jnp.mean(o * o, axis=-1, keepdims=True) + EPS)
            o_ref[it["b"], :, hl] = o * ng_ref[...] * (zz * jax.nn.sigmoid(zz))
        return carry

    lax.fori_loop(0, nbatch // GDN_GB, group, 0)


def _gdn(qkva, z, abc, alc, dtc, ng):
    batch, seq, _ = qkva.shape
    blk = lambda w: pl.BlockSpec((batch, CHUNK, w), lambda c: (0, c, 0))
    const = lambda shape: pl.BlockSpec(shape, lambda c: (0, 0))
    return pl.pallas_call(
        _gdn_kernel,
        grid=(seq // CHUNK,),
        in_specs=[blk(3 * GDN_W), blk(GDN_W), blk(128),
                  const((1, 128)), const((1, 128)), const((1, GDN_DIM))],
        out_specs=blk(GDN_W),
        out_shape=jax.ShapeDtypeStruct((batch, seq, GDN_W), F32),
        scratch_shapes=[pltpu.VMEM((batch * GDN_HEADS, GDN_DIM, GDN_DIM), F32)],
        compiler_params=_params("arbitrary"),
        name="gdn",
    )(qkva, z, abc, alc, dtc, ng)


def _dil_bias(delta, dilation, slopes):
    valid = (delta >= 0) & (delta <= SWA_BLOCK)
    df = delta.astype(F32) * LOG2E
    return jnp.concatenate([jnp.where(valid, -(s * float(dilation)) * df, NEG) for s in slopes], axis=0)


def _dil_tables(hp, b1, b4, b16):
    def iota(shape, dim):
        return lax.broadcasted_iota(jnp.int32, shape, dim)

    def slope(head, shape):
        e = jnp.full(shape, 126, jnp.int32) - head
        return lax.bitcast_convert_type(lax.shift_left(e, jnp.full_like(e, 23)), F32)

    wide = (SWA_BLOCK, 2 * SWA_BLOCK)
    sl = [slope(2 * hp, wide), slope(2 * hp + 1, wide)]
    p, c = iota(wide, 0), iota(wide, 1)
    d = 16 * ((p & 7) - (c & 15)) + ((p >> 3) - (c >> 4))
    b1[0] = _dil_bias(d + 16 * 8, 1, sl)
    b1[1] = _dil_bias(d, 1, sl)
    d = 4 * ((p & 31) - (c & 63)) + ((p >> 5) - (c >> 6))
    b4[0] = _dil_bias(d + 4 * 32, 4, sl)
    b4[1] = _dil_bias(d, 4, sl)
    sq = (SWA_BLOCK, SWA_BLOCK)
    b16[...] = _dil_bias(iota(sq, 0) - iota(sq, 1), 16, [slope(2 * hp, sq), slope(2 * hp + 1, sq)])


def _attend(tiles, head_a):
    scores = []
    for q, k, v, bias in tiles:
        qa = jnp.where(head_a, q, 0.0)
        qs = (jnp.concatenate([qa, q - qa], axis=0) * (SWA_DIM ** -0.5 * LOG2E)).astype(BF16)
        scores.append(_dot_nt(qs, k.astype(BF16)) + bias)
    probs = []
    for s in scores:
        m = jnp.max(s, axis=-1, keepdims=True)
        probs.append((jnp.exp2(s - m).astype(BF16), m))
    outs = []
    for (q, k, v, bias), (e, m) in zip(tiles, probs):
        v1 = jnp.concatenate([v.astype(BF16), jnp.ones(v.shape, BF16)], axis=1)
        ol = _dot(e, v1)
        top, bot = ol[:SWA_BLOCK], ol[SWA_BLOCK:]
        w = 2 * SWA_DIM
        outs.append((jnp.where(head_a, top[:, :w], bot[:, :w]), jnp.where(head_a, top[:, w:], bot[:, w:]),
                     jnp.where(head_a, m[:SWA_BLOCK], m[SWA_BLOCK:])))
    return outs


def _dil_kernel(q_ref, k_ref, v_ref, o_ref, b1, b4, b16, s1, s4, s16):
    hp = pl.program_id(0)

    @pl.when(pl.program_id(1) == 0)
    def _():
        _dil_tables(hp, b1, b4, b16)

    head_a = lax.broadcasted_iota(jnp.int32, (1, 2 * SWA_DIM), 1) < SWA_DIM
    u = DIL_UNROLL
    nblk = q_ref.shape[2] * RES // SWA_BLOCK

    def cat(ref, pieces):
        return jnp.concatenate([ref[0, r, rows, :] for r, rows in pieces], axis=0)

    def body1(it, carry):
        tiles, where = [], []
        for j in range(u):
            n = it * u + j
            qrows = pl.ds(pl.multiple_of(n * 8, 8), 8)
            krows = pl.ds(pl.multiple_of(jnp.maximum(n * 8 - 8, 0), 8), 16)
            bias = b1[0] if j else jnp.where(n == 0, b1[1], b1[0])
            tiles.append((cat(q_ref, [(r, qrows) for r in range(RES)]),
                          cat(k_ref, [(r, krows) for r in range(RES)]),
                          cat(v_ref, [(r, krows) for r in range(RES)]), bias))
            where.append(qrows)
        for qrows, stats in zip(where, _attend(tiles, head_a)):
            for i, val in enumerate(stats):
                for r in range(RES):
                    s1[i, r, qrows, :] = val[8 * r:8 * r + 8]
        return carry

    lax.fori_loop(0, nblk // u, body1, 0)

    def body4(it, carry):
        tiles, where = [], []
        for j in range(u):
            r4, n = it * (u // 4) + j // 4, j % 4
            qrows = slice(32 * n, 32 * n + 32)
            k0 = max(32 * n - 32, 0)
            krows = slice(k0, k0 + 64)
            grp = [r4 + 4 * g for g in range(4)]
            tiles.append((cat(q_ref, [(r, qrows) for r in grp]), cat(k_ref, [(r, krows) for r in grp]),
                          cat(v_ref, [(r, krows) for r in grp]), b4[0] if n else b4[1]))
            where.append((grp, qrows))
        for (grp, qrows), stats in zip(where, _attend(tiles, head_a)):
            for i, val in enumerate(stats):
                for g, r in enumerate(grp):
                    s4[i, r, qrows, :] = val[32 * g:32 * g + 32]
        return carry

    lax.fori_loop(0, nblk // u, body4, 0)

    def body16(it, carry):
        rs = [it * u + j for j in range(u)]
        tiles = [(q_ref[0, r], k_ref[0, r], v_ref[0, r], b16[...]) for r in rs]
        for r, stats in zip(rs, _attend(tiles, head_a)):
            for i, val in enumerate(stats):
                s16[i, r] = val
        return carry

    lax.fori_loop(0, RES // u, body16, 0)

    m = jnp.maximum(jnp.maximum(s1[2], s4[2]), s16[2])
    ws = [jnp.exp2(s[2] - m) for s in (s1, s4, s16)]
    num = sum(w * s[0] for w, s in zip(ws, (s1, s4, s16)))
    den = sum(w * s[1] for w, s in zip(ws, (s1, s4, s16)))
    o_ref[0] = num / den


def _dilated(qkvb):
    batch, _, sub, _ = qkvb.shape
    npair = SWA_HEADS // 2
    spec = lambda off: pl.BlockSpec((1, RES, sub, 2 * SWA_DIM), lambda p, b: (b, 0, 0, off + p))
    wide = (2 * SWA_BLOCK, 2 * SWA_BLOCK)
    res = pltpu.VMEM((3, RES, sub, 2 * SWA_DIM), F32)
    return pl.pallas_call(
        _dil_kernel,
        grid=(npair, batch),
        in_specs=[spec(0), spec(npair), spec(2 * npair)],
        out_specs=spec(0),
        out_shape=jax.ShapeDtypeStruct((batch, RES, sub, SWA_W), F32),
        scratch_shapes=[pltpu.VMEM((2,) + wide, F32), pltpu.VMEM((2,) + wide, F32),
                        pltpu.VMEM((2 * SWA_BLOCK, SWA_BLOCK), F32), res, res, res],
        compiler_params=_params("arbitrary", "arbitrary"),
        name="dilated_attn",
    )(qkvb, qkvb, qkvb)


def _memkv_kernel(m_ref, g_ref, wk_ref, wv_ref, k_ref, v_ref):
    mn = _rms(m_ref[...], g_ref[...]).astype(BF16)
    k_ref[...] = _dot(mn, wk_ref[...]).astype(BF16)
    v_ref[...] = _dot(mn, wv_ref[...]).astype(BF16)


def _memkv(mem, g, wk, wv):
    t = mem.shape[0]
    tm = min(TM, t)
    row = pl.BlockSpec((tm, D_MODEL), lambda i: (i, 0))
    return pl.pallas_call(
        _memkv_kernel,
        grid=(t // tm,),
        in_specs=[row, _resident((1, D_MODEL)), _resident((D_MODEL, D_MODEL)),
                  _resident((D_MODEL, D_MODEL))],
        out_specs=[row, row],
        out_shape=[jax.ShapeDtypeStruct((t, D_MODEL), BF16)] * 2,
        compiler_params=_params("parallel"),
        name="mem_kv",
    )(mem, g, wk, wv)


def _outmem_kernel(x_ref, oa_ref, ob_ref, woa_ref, wob_ref, gmix_ref, gpre_ref, wq_ref,
                   k_ref, v_ref, wo_ref, gpost_ref, o_ref, att, obn):
    tm = x_ref.shape[0]
    rs = tm // OUT_SUB
    subs = [slice(i * rs, (i + 1) * rs) for i in range(OUT_SUB)]
    mixes = [_dot(oa_ref[s, :].astype(BF16), woa_ref[...]) for s in subs]
    for l in range(SWA_W // 128):
        for r in range(RES):
            obn[l, pl.ds(r, tm // RES, stride=RES), :] = ob_ref[0, r, :, 128 * l:128 * l + 128]
    xs = []
    for s, mix in zip(subs, mixes):
        ob = jnp.concatenate([obn[l, s, :] for l in range(SWA_W // 128)], axis=1).astype(BF16)
        xs.append(x_ref[s, :] + _rms(mix + _dot(ob, wob_ref[...]), gmix_ref[...]))
    qs = [(_dot(_rms(x, gpre_ref[...]).astype(BF16), wq_ref[...]) * (MEM_DIM ** -0.5)).astype(BF16)
          for x in xs]
    for h in range(MEM_HEADS):
        hl = slice(h * MEM_DIM, (h + 1) * MEM_DIM)
        scores = [_dot_nt(q[:, hl], k_ref[:, hl]) for q in qs]
        for s, sc in zip(subs, scores):
            e = jnp.exp(sc - jnp.max(sc, axis=-1, keepdims=True))
            p = e / jnp.sum(e, axis=-1, keepdims=True)
            att[s, hl] = _dot(p.astype(BF16), v_ref[:, hl]).astype(BF16)
    cs = [_dot(att[s, :], wo_ref[...]) for s in subs]
    for s, x, c in zip(subs, xs, cs):
        o_ref[s, :] = x + _rms(c, gpost_ref[...])


def _outmem(x, oa, ob, woa, wob, gmix, gpre, wq, kmem, vmem, wo, gpost, seq, n_mem):
    t = x.shape[0]
    per_seq = seq // TM_OUT
    row = lambda w: pl.BlockSpec((TM_OUT, w), lambda i: (i, 0))
    kv = pl.BlockSpec((n_mem, D_MODEL), lambda i: (i // per_seq, 0))
    return pl.pallas_call(
        _outmem_kernel,
        grid=(t // TM_OUT,),
        in_specs=[row(D_MODEL), row(GDN_W),
                  pl.BlockSpec((1, RES, TM_OUT // RES, SWA_W), lambda i: (i // per_seq, 0, i % per_seq, 0)),
                  _resident((GDN_W, D_MODEL)), _resident((SWA_W, D_MODEL)), _resident((1, D_MODEL)),
                  _resident((1, D_MODEL)), _resident((D_MODEL, D_MODEL)), kv, kv,
                  _resident((D_MODEL, D_MODEL)), _resident((1, D_MODEL))],
        out_specs=row(D_MODEL),
        out_shape=jax.ShapeDtypeStruct((t, D_MODEL), F32),
        scratch_shapes=[pltpu.VMEM((TM_OUT, D_MODEL), BF16), pltpu.VMEM((SWA_W // 128, TM_OUT, 128), F32)],
        compiler_params=_params("parallel"),
        name="out_mem_attn",
    )(x, oa, ob, woa, wob, gmix, gpre, wq, kmem, vmem, wo, gpost)


def _layer(x, mem, p, batch, seq, n_mem):
    bf = lambda w: w.astype(BF16)
    row = lambda g: g.reshape(1, -1)
    x = _ffn(x, row(p["ffn1_pre_g"]), bf(p["ffn1_w_gate"]), bf(p["ffn1_w_up"]),
             bf(p["ffn1_w_down"]), row(p["ffn1_post_g"]))

    w_in = p["w_in"]
    o_z, o_a, o_b = 3 * GDN_W, 4 * GDN_W, 4 * GDN_W + 2 * GDN_HEADS
    wab = jnp.pad(w_in[:, o_a:o_b], ((0, 0), (0, 128 - 2 * GDN_HEADS)))
    cw = jnp.pad(p["gdn_conv_w"], ((0, 8 - GDN_CONV), (0, 0)))
    qkva, z, abc, qkvb = _proj(x, row(p["mix_pre_g"]), bf(w_in[:, :o_z]), bf(w_in[:, o_z:o_a]),
                               bf(wab), bf(w_in[:, o_b:]), cw, batch, seq)

    pad_lane = lambda v: jnp.pad(v.reshape(1, -1), ((0, 0), (0, 128 - GDN_HEADS)))
    per_seq = lambda a: a.reshape(batch, seq, a.shape[-1])
    o_gdn = _gdn(per_seq(qkva), per_seq(z), per_seq(abc), pad_lane(p["gdn_a_log"]),
                 pad_lane(p["gdn_dt_bias"]), row(p["gdn_norm_g"])).reshape(batch * seq, GDN_W)
    o_dil = _dilated(qkvb)

    kmem, vmem = _memkv(mem, row(p["mem_kv_g"]), bf(p["mem_wk"]), bf(p["mem_wv"]))
    w_out = bf(p["w_out"])
    x = _outmem(x, o_gdn, o_dil, w_out[:GDN_W], w_out[GDN_W:], row(p["mix_post_g"]),
                row(p["mem_pre_g"]), bf(p["mem_wq"]), kmem, vmem, bf(p["mem_wo"]),
                row(p["mem_post_g"]), seq, n_mem)

    return _ffn(x, row(p["ffn2_pre_g"]), bf(p["ffn2_w_gate"]), bf(p["ffn2_w_up"]),
                bf(p["ffn2_w_down"]), row(p["ffn2_post_g"]))


_NAMES = ("ffn1_pre_g", "ffn1_w_gate", "ffn1_w_up", "ffn1_w_down", "ffn1_post_g",
          "mix_pre_g", "w_in", "gdn_conv_w", "gdn_a_log", "gdn_dt_bias", "gdn_norm_g", "w_out",
          "mix_post_g", "mem_pre_g", "mem_kv_g", "mem_wq", "mem_wk", "mem_wv", "mem_wo",
          "mem_post_g", "ffn2_pre_g", "ffn2_w_gate", "ffn2_w_up", "ffn2_w_down", "ffn2_post_g")


def kernel(x, mem, ffn1_pre_g, ffn1_w_gate, ffn1_w_up, ffn1_w_down, ffn1_post_g, mix_pre_g, w_in, gdn_conv_w, gdn_a_log, gdn_dt_bias, gdn_norm_g, w_out, mix_post_g, mem_pre_g, mem_kv_g, mem_wq, mem_wk, mem_wv, mem_wo, mem_post_g, ffn2_pre_g, ffn2_w_gate, ffn2_w_up, ffn2_w_down, ffn2_post_g):
    stacked = dict(zip(_NAMES, (ffn1_pre_g, ffn1_w_gate, ffn1_w_up, ffn1_w_down, ffn1_post_g,
                                mix_pre_g, w_in, gdn_conv_w, gdn_a_log, gdn_dt_bias, gdn_norm_g, w_out,
                                mix_post_g, mem_pre_g, mem_kv_g, mem_wq, mem_wk, mem_wv, mem_wo,
                                mem_post_g, ffn2_pre_g, ffn2_w_gate, ffn2_w_up, ffn2_w_down,
                                ffn2_post_g)))
    batch, seq, _ = x.shape
    n_mem = mem.shape[1]
    xf = x.reshape(batch * seq, D_MODEL)
    memf = mem.reshape(batch * n_mem, D_MODEL)
    for l in range(ffn1_pre_g.shape[0]):
        xf = _layer(xf, memf, {k: v[l] for k, v in stacked.items()}, batch, seq, n_mem)
    return xf.reshape(batch, seq, D_MODEL)
```

```python
import functools

import jax
import jax.numpy as jnp
from jax import lax
from jax.experimental import pallas as pl
from jax.experimental.pallas import tpu as pltpu

F32 = jnp.float32
BF16 = jnp.bfloat16

D_MODEL = 1024
D_FF = 2816
EPS = 1e-6
GDN_HEADS = 4
GDN_DIM = 128
GDN_W = GDN_HEADS * GDN_DIM
GDN_CONV = 4
CHUNK = 64
SWA_HEADS = 8
SWA_DIM = 64
SWA_W = SWA_HEADS * SWA_DIM
SWA_BLOCK = 128
DILATIONS = (1, 4, 16)
RES = 16
MEM_HEADS = 4
MEM_DIM = D_MODEL // MEM_HEADS
NEG = -1e30
LOG2E = 1.4426950408889634

V7X_VMEM_BYTES = 64 * 1024 * 1024
VMEM_LIMIT = V7X_VMEM_BYTES - 8 * 1024 * 1024

TM = 1024
HALO = 16
TM_FFN = 1024
FFN_FC = 256
PROJ_TILE = 256
TM_OUT = 1024
OUT_SUB = 4
GDN_GB = 8
DIL_UNROLL = 8


def _rms(x, g):
    return x * lax.rsqrt(jnp.mean(x * x, axis=-1, keepdims=True) + EPS) * g


def _dot(a, b):
    return jnp.dot(a, b, preferred_element_type=F32)


def _dot_nt(a, b):
    return lax.dot_general(a, b, (((1,), (1,)), ((), ())), preferred_element_type=F32)


def _dot_tn(a, b):
    return lax.dot_general(a, b, (((0,), (0,)), ((), ())), preferred_element_type=F32)


def _resident(shape):
    return pl.BlockSpec(shape, lambda *_: (0,) * len(shape), pipeline_mode=pl.Buffered(1))


def _params(*sem):
    return pltpu.CompilerParams(dimension_semantics=sem, vmem_limit_bytes=VMEM_LIMIT)


def _ffn_kernel(x_ref, gpre_ref, wg_ref, wu_ref, wd_ref, gpost_ref, o_ref, a_ref):
    x = x_ref[...]
    xn = _rms(x, gpre_ref[...]).astype(BF16)
    for c in range(D_FF // FFN_FC):
        sl = slice(c * FFN_FC, (c + 1) * FFN_FC)
        g = _dot(xn, wg_ref[:, sl])
        u = _dot(xn, wu_ref[:, sl])
        a_ref[:, sl] = (g * jax.nn.sigmoid(g) * u).astype(BF16)
    f = _dot(a_ref[...], wd_ref[...])
    o_ref[...] = x + 0.5 * _rms(f, gpost_ref[...])


def _ffn(x, gpre, wg, wu, wd, gpost):
    t = x.shape[0]
    row = pl.BlockSpec((TM_FFN, D_MODEL), lambda i: (i, 0))
    return pl.pallas_call(
        _ffn_kernel,
        grid=(t // TM_FFN,),
        in_specs=[row, _resident((1, D_MODEL)), _resident((D_MODEL, D_FF)),
                  _resident((D_MODEL, D_FF)), _resident((D_FF, D_MODEL)),
                  _resident((1, D_MODEL))],
        out_specs=row,
        out_shape=jax.ShapeDtypeStruct((t, D_MODEL), F32),
        scratch_shapes=[pltpu.VMEM((TM_FFN, D_FF), BF16)],
        compiler_params=_params("parallel"),
        name="ffn",
    )(x, gpre, wg, wu, wd, gpost)


def _proj_kernel(x_ref, xh_ref, g_ref, wa_ref, wz_ref, wab_ref, wb_ref, cw_ref,
                 qkva_ref, z_ref, abc_ref, qkvb_ref, rb, *, per_seq):
    g = g_ref[...]
    h = _rms(x_ref[...], g).astype(BF16)
    keep = jnp.where(pl.program_id(0) % per_seq == 0, 0.0, 1.0)
    hx = jnp.concatenate([(_rms(xh_ref[...], g) * keep).astype(BF16), h], axis=0)

    def conv_tile(t, xa):
        for hd in range(PROJ_TILE // GDN_DIM):
            lanes = slice(PROJ_TILE * t + hd * GDN_DIM, PROJ_TILE * t + (hd + 1) * GDN_DIM)
            xh = xa[:, hd * GDN_DIM:(hd + 1) * GDN_DIM]
            y = cw_ref[0:1, lanes] * xh
            for j in range(1, GDN_CONV):
                y = pltpu.roll(y, 1, axis=0) + cw_ref[j:j + 1, lanes] * xh
            y = y[HALO:]
            y = y * jax.nn.sigmoid(y)
            if lanes.start < 2 * GDN_W:
                scale = GDN_DIM ** -0.5 if lanes.start < GDN_W else 1.0
                y = y * (lax.rsqrt(jnp.sum(y * y, axis=-1, keepdims=True) + EPS) * scale)
            qkva_ref[:, lanes] = y

    def attn_tile(t):
        yb = _dot(h, wb_ref[:, PROJ_TILE * t:PROJ_TILE * (t + 1)])
        for half in range(PROJ_TILE // 128):
            l = t * (PROJ_TILE // 128) + half
            rb[l] = yb[:, 128 * half:128 * half + 128]
            for r in range(RES):
                qkvb_ref[0, r, :, 128 * l:128 * l + 128] = rb[l, pl.ds(r, TM // RES, stride=RES), :]

    n_a = 3 * GDN_W // PROJ_TILE
    prev = None
    for t in range(n_a):
        xa = _dot(hx, wa_ref[:, PROJ_TILE * t:PROJ_TILE * (t + 1)])
        attn_tile(t)
        if t:
            conv_tile(t - 1, prev)
        prev = xa
    z_ref[...] = _dot(h, wz_ref[...])
    abc_ref[...] = _dot(h, wab_ref[...])
    conv_tile(n_a - 1, prev)


def _proj(x, g, wa, wz, wab, wb, cw, batch, seq):
    t = x.shape[0]
    per_seq = seq // TM
    row = lambda w: pl.BlockSpec((TM, w), lambda i: (i, 0))
    halo = pl.BlockSpec((HALO, D_MODEL), lambda i: (jnp.maximum(i * (TM // HALO) - 1, 0), 0))
    return pl.pallas_call(
        functools.partial(_proj_kernel, per_seq=per_seq),
        grid=(t // TM,),
        in_specs=[row(D_MODEL), halo, _resident((1, D_MODEL)), _resident(wa.shape),
                  _resident(wz.shape), _resident(wab.shape), _resident(wb.shape),
                  _resident(cw.shape)],
        out_specs=[row(3 * GDN_W), row(GDN_W), row(128),
                   pl.BlockSpec((1, RES, TM // RES, 3 * SWA_W),
                                lambda i: (i // per_seq, 0, i % per_seq, 0))],
        out_shape=[jax.ShapeDtypeStruct((t, 3 * GDN_W), F32),
                   jax.ShapeDtypeStruct((t, GDN_W), F32),
                   jax.ShapeDtypeStruct((t, 128), F32),
                   jax.ShapeDtypeStruct((batch, RES, seq // RES, 3 * SWA_W), F32)],
        scratch_shapes=[pltpu.VMEM((3 * SWA_W // 128, TM, 128), F32)],
        compiler_params=_params("parallel"),
        name="mixer_proj",
    )(x, x, g, wa, wz, wab, wb, cw)


def _softplus(x):
    return jnp.maximum(x, 0.0) + jnp.log(1.0 + jnp.exp(-jnp.abs(x)))


def _level_masks():
    row = lax.broadcasted_iota(jnp.int32, (CHUNK, CHUNK), 0)
    col = lax.broadcasted_iota(jnp.int32, (CHUNK, CHUNK), 1)
    sels = []
    for lb in range(CHUNK.bit_length() - 1):
        sels.append(((row >> (lb + 1)) == (col >> (lb + 1))) & (((row >> lb) & 1) == 1)
                    & (((col >> lb) & 1) == 0))
    return row, col, sels


def _unit_lower_inverse(a_list, eye, sels):
    ts = [eye - jnp.where(sels[0], a, 0.0) for a in a_list]
    for sel in sels[1:]:
        tb = [t.astype(BF16) for t in ts]
        ps = [_dot(jnp.where(sel, a, 0.0).astype(BF16), t) for a, t in zip(a_list, tb)]
        ts = [t - _dot(t16, p.astype(BF16)) for t, t16, p in zip(ts, tb, ps)]
    return ts


def _gdn_kernel(x_ref, z_ref, abc_ref, alc_ref, dtc_ref, ng_ref, o_ref, st):
    nbatch = x_ref.shape[0]

    @pl.when(pl.program_id(0) == 0)
    def _():
        st[...] = jnp.zeros_like(st)

    row, col, sels = _level_masks()
    tril = col <= row
    strict = col < row
    eye = jnp.where(row == col, 1.0, 0.0).astype(F32)
    lower = jnp.where(tril, 1.0, 0.0).astype(F32)
    upper = jnp.where(row <= col, 1.0, 0.0).astype(F32)
    hi = lax.Precision.HIGHEST

    def group(gi, carry):
        items = []
        for bb in range(GDN_GB):
            b = gi * GDN_GB + bb
            abc = abc_ref[b]
            g_col = -jnp.exp(alc_ref[...]) * _softplus(abc + dtc_ref[...])
            gc_col = jnp.dot(lower, g_col, precision=hi, preferred_element_type=F32)
            gc_row = lax.dot_general(g_col, upper, (((0,), (0,)), ((), ())), precision=hi,
                                     preferred_element_type=F32)
            beta_all = jax.nn.sigmoid(abc)
            for h in range(GDN_HEADS):
                items.append(dict(
                    b=b, h=h,
                    q=x_ref[b, :, h * GDN_DIM:(h + 1) * GDN_DIM],
                    k=x_ref[b, :, GDN_W + h * GDN_DIM:GDN_W + (h + 1) * GDN_DIM],
                    v=x_ref[b, :, 2 * GDN_W + h * GDN_DIM:2 * GDN_W + (h + 1) * GDN_DIM],
                    gcc=gc_col[:, h:h + 1], gcr=gc_row[h:h + 1, :],
                    gl=gc_col[CHUNK - 1:CHUNK, h:h + 1], beta=beta_all[:, 4 + h:5 + h]))

        kqs = []
        for it in items:
            it["kb"] = it["k"] * it["beta"]
            kqs.append(_dot_nt(jnp.concatenate([it["kb"], it["q"]], axis=0).astype(BF16),
                               it["k"].astype(BF16)))
        a_list = []
        for it, kq in zip(items, kqs):
            decay = jnp.exp(jnp.where(tril, it["gcc"] - it["gcr"], NEG))
            a_list.append(jnp.where(strict, kq[:CHUNK] * decay, 0.0))
            it["qk"] = jnp.where(tril, kq[CHUNK:] * decay, 0.0).astype(BF16)
        ts = _unit_lower_inverse(a_list, eye, sels)
        uws = []
        for it, t in zip(items, ts):
            eg = jnp.exp(it["gcc"])
            rhs = jnp.concatenate([it["v"] * it["beta"], it["kb"] * eg], axis=1).astype(BF16)
            uws.append(_dot(t.astype(BF16), rhs))
            it["qd"] = (it["q"] * eg).astype(BF16)
            it["kd"] = (it["k"] * jnp.exp(it["gl"] - it["gcc"])).astype(BF16)

        states, wss = [], []
        for it, uw in zip(items, uws):
            s = st[it["b"] * GDN_HEADS + it["h"]]
            states.append(s)
            lhs = jnp.concatenate([uw[:, GDN_DIM:].astype(BF16), it["qd"]], axis=0)
            wss.append(_dot(lhs, s.astype(BF16)))
        for it, uw, s, ws in zip(items, uws, states, wss):
            vb = (uw[:, :GDN_DIM] - ws[:CHUNK]).astype(BF16)
            o = ws[CHUNK:] + _dot(it["qk"], vb)
            st[it["b"] * GDN_HEADS + it["h"]] = s * jnp.exp(it["gl"]) + _dot_tn(it["kd"], vb)
            hl = slice(it["h"] * GDN_DIM, (it["h"] + 1) * GDN_DIM)
            zz = z_ref[it["b"], :, hl]
            o = o * lax.rsqrt(jnp.mean(o * o, axis=-1, keepdims=True) + EPS)
            o_ref[it["b"], :, hl] = o * ng_ref[...] * (zz * jax.nn.sigmoid(zz))
        return carry

    lax.fori_loop(0, nbatch // GDN_GB, group, 0)


def _gdn(qkva, z, abc, alc, dtc, ng):
    batch, seq, _ = qkva.shape
    blk = lambda w: pl.BlockSpec((batch, CHUNK, w), lambda c: (0, c, 0))
    const = lambda shape: pl.BlockSpec(shape, lambda c: (0, 0))
    return pl.pallas_call(
        _gdn_kernel,
        grid=(seq // CHUNK,),
        in_specs=[blk(3 * GDN_W), blk(GDN_W), blk(128),
                  const((1, 128)), const((1, 128)), const((1, GDN_DIM))],
        out_specs=blk(GDN_W),
        out_shape=jax.ShapeDtypeStruct((batch, seq, GDN_W), F32),
        scratch_shapes=[pltpu.VMEM((batch * GDN_HEADS, GDN_DIM, GDN_DIM), F32)],
        compiler_params=_params("arbitrary"),
        name="gdn",
    )(qkva, z, abc, alc, dtc, ng)


def _dil_bias(delta, dilation, slopes):
    valid = (delta >= 0) & (delta <= SWA_BLOCK)
    df = delta.astype(F32) * LOG2E
    return jnp.concatenate([jnp.where(valid, -(s * float(dilation)) * df, NEG) for s in slopes], axis=0)


def _dil_tables(hp, b1, b4, b16):
    def iota(shape, dim):
        return lax.broadcasted_iota(jnp.int32, shape, dim)

    def slope(head, shape):
        e = jnp.full(shape, 126, jnp.int32) - head
        return lax.bitcast_convert_type(lax.shift_left(e, jnp.full_like(e, 23)), F32)

    wide = (SWA_BLOCK, 2 * SWA_BLOCK)
    sl = [slope(2 * hp, wide), slope(2 * hp + 1, wide)]
    p, c = iota(wide, 0), iota(wide, 1)
    d = 16 * ((p & 7) - (c & 15)) + ((p >> 3) - (c >> 4))
    b1[0] = _dil_bias(d + 16 * 8, 1, sl)
    b1[1] = _dil_bias(d, 1, sl)
    d = 4 * ((p & 31) - (c & 63)) + ((p >> 5) - (c >> 6))
    b4[0] = _dil_bias(d + 4 * 32, 4, sl)
    b4[1] = _dil_bias(d, 4, sl)
    sq = (SWA_BLOCK, SWA_BLOCK)
    b16[...] = _dil_bias(iota(sq, 0) - iota(sq, 1), 16, [slope(2 * hp, sq), slope(2 * hp + 1, sq)])


def _attend(tiles, head_a):
    scores = []
    for q, k, v, bias in tiles:
        qa = jnp.where(head_a, q, 0.0)
        qs = (jnp.concatenate([qa, q - qa], axis=0) * (SWA_DIM ** -0.5 * LOG2E)).astype(BF16)
        scores.append(_dot_nt(qs, k.astype(BF16)) + bias)
    probs = []
    for s in scores:
        m = jnp.max(s, axis=-1, keepdims=True)
        probs.append((jnp.exp2(s - m).astype(BF16), m))
    outs = []
    for (q, k, v, bias), (e, m) in zip(tiles, probs):
        v1 = jnp.concatenate([v.astype(BF16), jnp.ones(v.shape, BF16)], axis=1)
        ol = _dot(e, v1)
        top, bot = ol[:SWA_BLOCK], ol[SWA_BLOCK:]
        w = 2 * SWA_DIM
        outs.append((jnp.where(head_a, top[:, :w], bot[:, :w]), jnp.where(head_a, top[:, w:], bot[:, w:]),
                     jnp.where(head_a, m[:SWA_BLOCK], m[SWA_BLOCK:])))
    return outs


def _dil_kernel(q_ref, k_ref, v_ref, o_ref, b1, b4, b16, s1, s4, s16):
    hp = pl.program_id(0)

    @pl.when(pl.program_id(1) == 0)
    def _():
        _dil_tables(hp, b1, b4, b16)

    head_a = lax.broadcasted_iota(jnp.int32, (1, 2 * SWA_DIM), 1) < SWA_DIM
    u = DIL_UNROLL
    nblk = q_ref.shape[2] * RES // SWA_BLOCK

    def cat(ref, pieces):
        return jnp.concatenate([ref[0, r, rows, :] for r, rows in pieces], axis=0)

    def body1(it, carry):
        tiles, where = [], []
        for j in range(u):
            n = it * u + j
            qrows = pl.ds(pl.multiple_of(n * 8, 8), 8)
            krows = pl.ds(pl.multiple_of(jnp.maximum(n * 8 - 8, 0), 8), 16)
            bias = b1[0] if j else jnp.where(n == 0, b1[1], b1[0])
            tiles.append((cat(q_ref, [(r, qrows) for r in range(RES)]),
                          cat(k_ref, [(r, krows) for r in range(RES)]),
                          cat(v_ref, [(r, krows) for r in range(RES)]), bias))
            where.append(qrows)
        for qrows, stats in zip(where, _attend(tiles, head_a)):
            for i, val in enumerate(stats):
                for r in range(RES):
                    s1[i, r, qrows, :] = val[8 * r:8 * r + 8]
        return carry

    lax.fori_loop(0, nblk // u, body1, 0)

    def body4(it, carry):
        tiles, where = [], []
        for j in range(u):
            r4, n = it * (u // 4) + j // 4, j % 4
            qrows = slice(32 * n, 32 * n + 32)
            k0 = max(32 * n - 32, 0)
            krows = slice(k0, k0 + 64)
            grp = [r4 + 4 * g for g in range(4)]
            tiles.append((cat(q_ref, [(r, qrows) for r in grp]), cat(k_ref, [(r, krows) for r in grp]),
                          cat(v_ref, [(r, krows) for r in grp]), b4[0] if n else b4[1]))
            where.append((grp, qrows))
        for (grp, qrows), stats in zip(where, _attend(tiles, head_a)):
            for i, val in enumerate(stats):
                for g, r in enumerate(grp):
                    s4[i, r, qrows, :] = val[32 * g:32 * g + 32]
        return carry

    lax.fori_loop(0, nblk // u, body4, 0)

    def body16(it, carry):
        rs = [it * u + j for j in range(u)]
        tiles = [(q_ref[0, r], k_ref[0, r], v_ref[0, r], b16[...]) for r in rs]
        for r, stats in zip(rs, _attend(tiles, head_a)):
            for i, val in enumerate(stats):
                s16[i, r] = val
        return carry

    lax.fori_loop(0, RES // u, body16, 0)

    m = jnp.maximum(jnp.maximum(s1[2], s4[2]), s16[2])
    ws = [jnp.exp2(s[2] - m) for s in (s1, s4, s16)]
    num = sum(w * s[0] for w, s in zip(ws, (s1, s4, s16)))
    den = sum(w * s[1] for w, s in zip(ws, (s1, s4, s16)))
    o_ref[0] = num / den


def _dilated(qkvb):
    batch, _, sub, _ = qkvb.shape
    npair = SWA_HEADS // 2
    spec = lambda off: pl.BlockSpec((1, RES, sub, 2 * SWA_DIM), lambda p, b: (b, 0, 0, off + p))
    wide = (2 * SWA_BLOCK, 2 * SWA_BLOCK)
    res = pltpu.VMEM((3, RES, sub, 2 * SWA_DIM), F32)
    return pl.pallas_call(
        _dil_kernel,
        grid=(npair, batch),
        in_specs=[spec(0), spec(npair), spec(2 * npair)],
        out_specs=spec(0),
        out_shape=jax.ShapeDtypeStruct((batch, RES, sub, SWA_W), F32),
        scratch_shapes=[pltpu.VMEM((2,) + wide, F32), pltpu.VMEM((2,) + wide, F32),
                        pltpu.VMEM((2 * SWA_BLOCK, SWA_BLOCK), F32), res, res, res],
        compiler_params=_params("arbitrary", "arbitrary"),
        name="dilated_attn",
    )(qkvb, qkvb, qkvb)


def _memkv_kernel(m_ref, g_ref, wk_ref, wv_ref, k_ref, v_ref):
    mn = _rms(m_ref[...], g_ref[...]).astype(BF16)
    k_ref[...] = _dot(mn, wk_ref[...]).astype(BF16)
    v_ref[...] = _dot(mn, wv_ref[...]).astype(BF16)


def _memkv(mem, g, wk, wv):
    t = mem.shape[0]
    tm = min(TM, t)
    row = pl.BlockSpec((tm, D_MODEL), lambda i: (i, 0))
    return pl.pallas_call(
        _memkv_kernel,
        grid=(t // tm,),
        in_specs=[row, _resident((1, D_MODEL)), _resident((D_MODEL, D_MODEL)),
                  _resident((D_MODEL, D_MODEL))],
        out_specs=[row, row],
        out_shape=[jax.ShapeDtypeStruct((t, D_MODEL), BF16)] * 2,
        compiler_params=_params("parallel"),
        name="mem_kv",
    )(mem, g, wk, wv)


def _outmem_kernel(x_ref, oa_ref, ob_ref, woa_ref, wob_ref, gmix_ref, gpre_ref, wq_ref,
                   k_ref, v_ref, wo_ref, gpost_ref, o_ref, att, obn):
    tm = x_ref.shape[0]
    rs = tm // OUT_SUB
    subs = [slice(i * rs, (i + 1) * rs) for i in range(OUT_SUB)]
    mixes = [_dot(oa_ref[s, :].astype(BF16), woa_ref[...]) for s in subs]
    for l in range(SWA_W // 128):
        for r in range(RES):
            obn[l, pl.ds(r, tm // RES, stride=RES), :] = ob_ref[0, r, :, 128 * l:128 * l + 128]
    xs = []
    for s, mix in zip(subs, mixes):
        ob = jnp.concatenate([obn[l, s, :] for l in range(SWA_W // 128)], axis=1).astype(BF16)
        xs.append(x_ref[s, :] + _rms(mix + _dot(ob, wob_ref[...]), gmix_ref[...]))
    qs = [(_dot(_rms(x, gpre_ref[...]).astype(BF16), wq_ref[...]) * (MEM_DIM ** -0.5)).astype(BF16)
          for x in xs]
    for h in range(MEM_HEADS):
        hl = slice(h * MEM_DIM, (h + 1) * MEM_DIM)
        scores = [_dot_nt(q[:, hl], k_ref[:, hl]) for q in qs]
        for s, sc in zip(subs, scores):
            e = jnp.exp(sc - jnp.max(sc, axis=-1, keepdims=True))
            p = e / jnp.sum(e, axis=-1, keepdims=True)
            att[s, hl] = _dot(p.astype(BF16), v_ref[:, hl]).astype(BF16)
    cs = [_dot(att[s, :], wo_ref[...]) for s in subs]
    for s, x, c in zip(subs, xs, cs):
        o_ref[s, :] = x + _rms(c, gpost_ref[...])


def _outmem(x, oa, ob, woa, wob, gmix, gpre, wq, kmem, vmem, wo, gpost, seq, n_mem):
    t = x.shape[0]
    per_seq = seq // TM_OUT
    row = lambda w: pl.BlockSpec((TM_OUT, w), lambda i: (i, 0))
    kv = pl.BlockSpec((n_mem, D_MODEL), lambda i: (i // per_seq, 0))
    return pl.pallas_call(
        _outmem_kernel,
        grid=(t // TM_OUT,),
        in_specs=[row(D_MODEL), row(GDN_W),
                  pl.BlockSpec((1, RES, TM_OUT // RES, SWA_W), lambda i: (i // per_seq, 0, i % per_seq, 0)),
                  _resident((GDN_W, D_MODEL)), _resident((SWA_W, D_MODEL)), _resident((1, D_MODEL)),
                  _resident((1, D_MODEL)), _resident((D_MODEL, D_MODEL)), kv, kv,
                  _resident((D_MODEL, D_MODEL)), _resident((1, D_MODEL))],
        out_specs=row(D_MODEL),
        out_shape=jax.ShapeDtypeStruct((t, D_MODEL), F32),
        scratch_shapes=[pltpu.VMEM((TM_OUT, D_MODEL), BF16), pltpu.VMEM((SWA_W // 128, TM_OUT, 128), F32)],
        compiler_params=_params("parallel"),
        name="out_mem_attn",
    )(x, oa, ob, woa, wob, gmix, gpre, wq, kmem, vmem, wo, gpost)


def _layer(x, mem, p, batch, seq, n_mem):
    bf = lambda w: w.astype(BF16)
    row = lambda g: g.reshape(1, -1)
    x = _ffn(x, row(p["ffn1_pre_g"]), bf(p["ffn1_w_gate"]), bf(p["ffn1_w_up"]),
             bf(p["ffn1_w_down"]), row(p["ffn1_post_g"]))

    w_in = p["w_in"]
    o_z, o_a, o_b = 3 * GDN_W, 4 * GDN_W, 4 * GDN_W + 2 * GDN_HEADS
    wab = jnp.pad(w_in[:, o_a:o_b], ((0, 0), (0, 128 - 2 * GDN_HEADS)))
    cw = jnp.pad(p["gdn_conv_w"], ((0, 8 - GDN_CONV), (0, 0)))
    qkva, z, abc, qkvb = _proj(x, row(p["mix_pre_g"]), bf(w_in[:, :o_z]), bf(w_in[:, o_z:o_a]),
                               bf(wab), bf(w_in[:, o_b:]), cw, batch, seq)

    pad_lane = lambda v: jnp.pad(v.reshape(1, -1), ((0, 0), (0, 128 - GDN_HEADS)))
    per_seq = lambda a: a.reshape(batch, seq, a.shape[-1])
    o_gdn = _gdn(per_seq(qkva), per_seq(z), per_seq(abc), pad_lane(p["gdn_a_log"]),
                 pad_lane(p["gdn_dt_bias"]), row(p["gdn_norm_g"])).reshape(batch * seq, GDN_W)
    o_dil = _dilated(qkvb)

    kmem, vmem = _memkv(mem, row(p["mem_kv_g"]), bf(p["mem_wk"]), bf(p["mem_wv"]))
    w_out = bf(p["w_out"])
    x = _outmem(x, o_gdn, o_dil, w_out[:GDN_W], w_out[GDN_W:], row(p["mix_post_g"]),
                row(p["mem_pre_g"]), bf(p["mem_wq"]), kmem, vmem, bf(p["mem_wo"]),
                row(p["mem_post_g"]), seq, n_mem)

    return _ffn(x, row(p["ffn2_pre_g"]), bf(p["ffn2_w_gate"]), bf(p["ffn2_w_up"]),
                bf(p["ffn2_w_down"]), row(p["ffn2_post_g"]))


_NAMES = ("ffn1_pre_g", "ffn1_w_gate", "ffn1_w_up", "ffn1_w_down", "ffn1_post_g",
          "mix_pre_g", "w_in", "gdn_conv_w", "gdn_a_log", "gdn_dt_bias", "gdn_norm_g", "w_out",
          "mix_post_g", "mem_pre_g", "mem_kv_g", "mem_wq", "mem_wk", "mem_wv", "mem_wo",
          "mem_post_g", "ffn2_pre_g", "ffn2_w_gate", "ffn2_w_up", "ffn2_w_down", "ffn2_post_g")


def kernel(x, mem, ffn1_pre_g, ffn1_w_gate, ffn1_w_up, ffn1_w_down, ffn1_post_g, mix_pre_g, w_in, gdn_conv_w, gdn_a_log, gdn_dt_bias, gdn_norm_g, w_out, mix_post_g, mem_pre_g, mem_kv_g, mem_wq, mem_wk, mem_wv, mem_wo, mem_post_g, ffn2_pre_g, ffn2_w_gate, ffn2_w_up, ffn2_w_down, ffn2_post_g):
    stacked = dict(zip(_NAMES, (ffn1_pre_g, ffn1_w_gate, ffn1_w_up, ffn1_w_down, ffn1_post_g,
                                mix_pre_g, w_in, gdn_conv_w, gdn_a_log, gdn_dt_bias, gdn_norm_g, w_out,
                                mix_post_g, mem_pre_g, mem_kv_g, mem_wq, mem_wk, mem_wv, mem_wo,
                                mem_post_g, ffn2_pre_g, ffn2_w_gate, ffn2_w_up, ffn2_w_down,
                                ffn2_post_g)))
    batch, seq, _ = x.shape
    n_mem = mem.shape[1]
    xf = x.reshape(batch * seq, D_MODEL)
    memf = mem.reshape(batch * n_mem, D_MODEL)
    for l in range(ffn1_pre_g.shape[0]):
        xf = _layer(xf, memf, {k: v[l] for k, v in stacked.items()}, batch, seq, n_mem)
    return xf.reshape(batch, seq, D_MODEL)
```

```python
import functools

import jax
import jax.numpy as jnp
from jax import lax
from jax.experimental import pallas as pl
from jax.experimental.pallas import tpu as pltpu

F32 = jnp.float32
BF16 = jnp.bfloat16

D_MODEL = 1024
D_FF = 2816
EPS = 1e-6
GDN_HEADS = 4
GDN_DIM = 128
GDN_W = GDN_HEADS * GDN_DIM
GDN_CONV = 4
CHUNK = 64
SWA_HEADS = 8
SWA_DIM = 64
SWA_W = SWA_HEADS * SWA_DIM
SWA_BLOCK = 128
DILATIONS = (1, 4, 16)
RES = 16
MEM_HEADS = 4
MEM_DIM = D_MODEL // MEM_HEADS
NEG = -1e30
LOG2E = 1.4426950408889634

V7X_VMEM_BYTES = 64 * 1024 * 1024
VMEM_LIMIT = V7X_VMEM_BYTES - 8 * 1024 * 1024

TM = 512
HALO = 16
TM_FFN = 512
FFN_FC = 256
PROJ_TILE = 256
TM_OUT = 1024
OUT_SUB = 4
GDN_GB = 8
DIL_UNROLL = 8


def _rms(x, g):
    return x * lax.rsqrt(jnp.mean(x * x, axis=-1, keepdims=True) + EPS) * g


def _dot(a, b):
    return jnp.dot(a, b, preferred_element_type=F32)


def _dot_nt(a, b):
    return lax.dot_general(a, b, (((1,), (1,)), ((), ())), preferred_element_type=F32)


def _dot_tn(a, b):
    return lax.dot_general(a, b, (((0,), (0,)), ((), ())), preferred_element_type=F32)


def _resident(shape):
    return pl.BlockSpec(shape, lambda *_: (0,) * len(shape), pipeline_mode=pl.Buffered(1))


def _params(*sem):
    return pltpu.CompilerParams(dimension_semantics=sem, vmem_limit_bytes=VMEM_LIMIT)


def _ffn_kernel(x_ref, gpre_ref, wg_ref, wu_ref, wd_ref, gpost_ref, o_ref, a_ref):
    x = x_ref[...]
    xn = _rms(x, gpre_ref[...]).astype(BF16)
    for c in range(D_FF // FFN_FC):
        sl = slice(c * FFN_FC, (c + 1) * FFN_FC)
        g = _dot(xn, wg_ref[:, sl].astype(BF16))
        u = _dot(xn, wu_ref[:, sl].astype(BF16))
        a_ref[:, sl] = (g * jax.nn.sigmoid(g) * u).astype(BF16)
    f = _dot(a_ref[...], wd_ref[...].astype(BF16))
    o_ref[...] = x + 0.5 * _rms(f, gpost_ref[...])


def _ffn(x, gpre, wg, wu, wd, gpost):
    t = x.shape[0]
    row = pl.BlockSpec((TM_FFN, D_MODEL), lambda i: (i, 0))
    return pl.pallas_call(
        _ffn_kernel,
        grid=(t // TM_FFN,),
        in_specs=[row, _resident((1, D_MODEL)), _resident((D_MODEL, D_FF)),
                  _resident((D_MODEL, D_FF)), _resident((D_FF, D_MODEL)),
                  _resident((1, D_MODEL))],
        out_specs=row,
        out_shape=jax.ShapeDtypeStruct((t, D_MODEL), F32),
        scratch_shapes=[pltpu.VMEM((TM_FFN, D_FF), BF16)],
        compiler_params=_params("parallel"),
        name="ffn",
    )(x, gpre, wg, wu, wd, gpost)


def _proj_kernel(x_ref, xh_ref, g_ref, wa_ref, wz_ref, wab_ref, wb_ref, cw_ref,
                 qkva_ref, z_ref, abc_ref, qkvb_ref, rb, *, per_seq):
    g = g_ref[...]
    h = _rms(x_ref[...], g).astype(BF16)
    keep = jnp.where(pl.program_id(0) % per_seq == 0, 0.0, 1.0)
    hx = jnp.concatenate([(_rms(xh_ref[...], g) * keep).astype(BF16), h], axis=0)

    def conv_tile(t, xa):
        for hd in range(PROJ_TILE // GDN_DIM):
            lanes = slice(PROJ_TILE * t + hd * GDN_DIM, PROJ_TILE * t + (hd + 1) * GDN_DIM)
            xh = xa[:, hd * GDN_DIM:(hd + 1) * GDN_DIM]
            y = cw_ref[0:1, lanes] * xh
            for j in range(1, GDN_CONV):
                y = pltpu.roll(y, 1, axis=0) + cw_ref[j:j + 1, lanes] * xh
            y = y[HALO:]
            y = y * jax.nn.sigmoid(y)
            if lanes.start < 2 * GDN_W:
                scale = GDN_DIM ** -0.5 if lanes.start < GDN_W else 1.0
                y = y * (lax.rsqrt(jnp.sum(y * y, axis=-1, keepdims=True) + EPS) * scale)
            qkva_ref[:, lanes] = y

    def attn_tile(t):
        yb = _dot(h, wb_ref[:, PROJ_TILE * t:PROJ_TILE * (t + 1)])
        for half in range(PROJ_TILE // 128):
            l = t * (PROJ_TILE // 128) + half
            rb[l] = yb[:, 128 * half:128 * half + 128]
            for r in range(RES):
                qkvb_ref[0, r, :, 128 * l:128 * l + 128] = rb[l, pl.ds(r, TM // RES, stride=RES), :]

    n_a = 3 * GDN_W // PROJ_TILE
    prev = None
    for t in range(n_a):
        xa = _dot(hx, wa_ref[:, PROJ_TILE * t:PROJ_TILE * (t + 1)])
        attn_tile(t)
        if t:
            conv_tile(t - 1, prev)
        prev = xa
    z_ref[...] = _dot(h, wz_ref[...])
    abc_ref[...] = _dot(h, wab_ref[...])
    conv_tile(n_a - 1, prev)


def _proj(x, g, wa, wz, wab, wb, cw, batch, seq):
    t = x.shape[0]
    per_seq = seq // TM
    row = lambda w: pl.BlockSpec((TM, w), lambda i: (i, 0))
    halo = pl.BlockSpec((HALO, D_MODEL), lambda i: (jnp.maximum(i * (TM // HALO) - 1, 0), 0))
    return pl.pallas_call(
        functools.partial(_proj_kernel, per_seq=per_seq),
        grid=(t // TM,),
        in_specs=[row(D_MODEL), halo, _resident((1, D_MODEL)), _resident(wa.shape),
                  _resident(wz.shape), _resident(wab.shape), _resident(wb.shape),
                  _resident(cw.shape)],
        out_specs=[row(3 * GDN_W), row(GDN_W), row(128),
                   pl.BlockSpec((1, RES, TM // RES, 3 * SWA_W),
                                lambda i: (i // per_seq, 0, i % per_seq, 0))],
        out_shape=[jax.ShapeDtypeStruct((t, 3 * GDN_W), F32),
                   jax.ShapeDtypeStruct((t, GDN_W), F32),
                   jax.ShapeDtypeStruct((t, 128), F32),
                   jax.ShapeDtypeStruct((batch, RES, seq // RES, 3 * SWA_W), F32)],
        scratch_shapes=[pltpu.VMEM((3 * SWA_W // 128, TM, 128), F32)],
        compiler_params=_params("parallel"),
        name="mixer_proj",
    )(x, x, g, wa, wz, wab, wb, cw)


def _softplus(x):
    return jnp.maximum(x, 0.0) + jnp.log(1.0 + jnp.exp(-jnp.abs(x)))


def _level_masks():
    row = lax.broadcasted_iota(jnp.int32, (CHUNK, CHUNK), 0)
    col = lax.broadcasted_iota(jnp.int32, (CHUNK, CHUNK), 1)
    sels = []
    for lb in range(CHUNK.bit_length() - 1):
        sels.append(((row >> (lb + 1)) == (col >> (lb + 1))) & (((row >> lb) & 1) == 1)
                    & (((col >> lb) & 1) == 0))
    return row, col, sels


def _unit_lower_inverse(a_list, eye, sels):
    ts = [eye - jnp.where(sels[0], a, 0.0) for a in a_list]
    for sel in sels[1:]:
        tb = [t.astype(BF16) for t in ts]
        ps = [_dot(jnp.where(sel, a, 0.0).astype(BF16), t) for a, t in zip(a_list, tb)]
        ts = [t - _dot(t16, p.astype(BF16)) for t, t16, p in zip(ts, tb, ps)]
    return ts


def _gdn_kernel(x_ref, z_ref, abc_ref, alc_ref, dtc_ref, ng_ref, o_ref, st):
    nbatch = x_ref.shape[0]

    @pl.when(pl.program_id(0) == 0)
    def _():
        st[...] = jnp.zeros_like(st)

    row, col, sels = _level_masks()
    tril = col <= row
    strict = col < row
    eye = jnp.where(row == col, 1.0, 0.0).astype(F32)
    lower = jnp.where(tril, 1.0, 0.0).astype(F32)
    upper = jnp.where(row <= col, 1.0, 0.0).astype(F32)
    hi = lax.Precision.HIGHEST

    def group(gi, carry):
        items = []
        for bb in range(GDN_GB):
            b = gi * GDN_GB + bb
            abc = abc_ref[b]
            g_col = -jnp.exp(alc_ref[...]) * _softplus(abc + dtc_ref[...])
            gc_col = jnp.dot(lower, g_col, precision=hi, preferred_element_type=F32)
            gc_row = lax.dot_general(g_col, upper, (((0,), (0,)), ((), ())), precision=hi,
                                     preferred_element_type=F32)
            beta_all = jax.nn.sigmoid(abc)
            for h in range(GDN_HEADS):
                items.append(dict(
                    b=b, h=h,
                    q=x_ref[b, :, h * GDN_DIM:(h + 1) * GDN_DIM],
                    k=x_ref[b, :, GDN_W + h * GDN_DIM:GDN_W + (h + 1) * GDN_DIM],
                    v=x_ref[b, :, 2 * GDN_W + h * GDN_DIM:2 * GDN_W + (h + 1) * GDN_DIM],
                    gcc=gc_col[:, h:h + 1], gcr=gc_row[h:h + 1, :],
                    gl=gc_col[CHUNK - 1:CHUNK, h:h + 1], beta=beta_all[:, 4 + h:5 + h]))

        kqs = []
        for it in items:
            it["kb"] = it["k"] * it["beta"]
            kqs.append(_dot_nt(jnp.concatenate([it["kb"], it["q"]], axis=0).astype(BF16),
                               it["k"].astype(BF16)))
        a_list = []
        for it, kq in zip(items, kqs):
            decay = jnp.exp(jnp.where(tril, it["gcc"] - it["gcr"], NEG))
            a_list.append(jnp.where(strict, kq[:CHUNK] * decay, 0.0))
            it["qk"] = jnp.where(tril, kq[CHUNK:] * decay, 0.0).astype(BF16)
        ts = _unit_lower_inverse(a_list, eye, sels)
        uws = []
        for it, t in zip(items, ts):
            eg = jnp.exp(it["gcc"])
            rhs = jnp.concatenate([it["v"] * it["beta"], it["kb"] * eg], axis=1).astype(BF16)
            uws.append(_dot(t.astype(BF16), rhs))
            it["qd"] = (it["q"] * eg).astype(BF16)
            it["kd"] = (it["k"] * jnp.exp(it["gl"] - it["gcc"])).astype(BF16)

        states, wss = [], []
        for it, uw in zip(items, uws):
            s = st[it["b"] * GDN_HEADS + it["h"]]
            states.append(s)
            lhs = jnp.concatenate([uw[:, GDN_DIM:].astype(BF16), it["qd"]], axis=0)
            wss.append(_dot(lhs, s.astype(BF16)))
        for it, uw, s, ws in zip(items, uws, states, wss):
            vb = (uw[:, :GDN_DIM] - ws[:CHUNK]).astype(BF16)
            o = ws[CHUNK:] + _dot(it["qk"], vb)
            st[it["b"] * GDN_HEADS + it["h"]] = s * jnp.exp(it["gl"]) + _dot_tn(it["kd"], vb)
            hl = slice(it["h"] * GDN_DIM, (it["h"] + 1) * GDN_DIM)
            zz = z_ref[it["b"], :, hl]
            o = o * lax.rsqrt(jnp.mean(o * o, axis=-1, keepdims=True) + EPS)
            o_ref[it["b"], :, hl] = o * ng_ref[...] * (zz * jax.nn.sigmoid(zz))
        return carry

    lax.fori_loop(0, nbatch // GDN_GB, group, 0)


def _gdn(qkva, z, abc, alc, dtc, ng):
    batch, seq, _ = qkva.shape
    blk = lambda w: pl.BlockSpec((batch, CHUNK, w), lambda c: (0, c, 0))
    const = lambda shape: pl.BlockSpec(shape, lambda c: (0, 0))
    return pl.pallas_call(
        _gdn_kernel,
        grid=(seq // CHUNK,),
        in_specs=[blk(3 * GDN_W), blk(GDN_W), blk(128),
                  const((1, 128)), const((1, 128)), const((1, GDN_DIM))],
        out_specs=blk(GDN_W),
        out_shape=jax.ShapeDtypeStruct((batch, seq, GDN_W), F32),
        scratch_shapes=[pltpu.VMEM((batch * GDN_HEADS, GDN_DIM, GDN_DIM), F32)],
        compiler_params=_params("arbitrary"),
        name="gdn",
    )(qkva, z, abc, alc, dtc, ng)


def _dil_bias(delta, dilation, slopes):
    valid = (delta >= 0) & (delta <= SWA_BLOCK)
    df = delta.astype(F32) * LOG2E
    return jnp.concatenate([jnp.where(valid, -(s * float(dilation)) * df, NEG) for s in slopes], axis=0)


def _dil_tables(hp, b1, b4, b16):
    def iota(shape, dim):
        return lax.broadcasted_iota(jnp.int32, shape, dim)

    def slope(head, shape):
        e = jnp.full(shape, 126, jnp.int32) - head
        return lax.bitcast_convert_type(lax.shift_left(e, jnp.full_like(e, 23)), F32)

    wide = (SWA_BLOCK, 2 * SWA_BLOCK)
    sl = [slope(2 * hp, wide), slope(2 * hp + 1, wide)]
    p, c = iota(wide, 0), iota(wide, 1)
    d = 16 * ((p & 7) - (c & 15)) + ((p >> 3) - (c >> 4))
    b1[0] = _dil_bias(d + 16 * 8, 1, sl)
    b1[1] = _dil_bias(d, 1, sl)
    d = 4 * ((p & 31) - (c & 63)) + ((p >> 5) - (c >> 6))
    b4[0] = _dil_bias(d + 4 * 32, 4, sl)
    b4[1] = _dil_bias(d, 4, sl)
    sq = (SWA_BLOCK, SWA_BLOCK)
    b16[...] = _dil_bias(iota(sq, 0) - iota(sq, 1), 16, [slope(2 * hp, sq), slope(2 * hp + 1, sq)])


def _attend(tiles, head_a):
    scores = []
    for q, k, v, bias in tiles:
        qa = jnp.where(head_a, q, 0.0)
        qs = (jnp.concatenate([qa, q - qa], axis=0) * (SWA_DIM ** -0.5 * LOG2E)).astype(BF16)
        scores.append(_dot_nt(qs, k.astype(BF16)) + bias)
    probs = []
    for s in scores:
        m = jnp.max(s, axis=-1, keepdims=True)
        probs.append((jnp.exp2(s - m).astype(BF16), m))
    outs = []
    for (q, k, v, bias), (e, m) in zip(tiles, probs):
        v1 = jnp.concatenate([v.astype(BF16), jnp.ones(v.shape, BF16)], axis=1)
        ol = _dot(e, v1)
        top, bot = ol[:SWA_BLOCK], ol[SWA_BLOCK:]
        w = 2 * SWA_DIM
        outs.append((jnp.where(head_a, top[:, :w], bot[:, :w]), jnp.where(head_a, top[:, w:], bot[:, w:]),
                     jnp.where(head_a, m[:SWA_BLOCK], m[SWA_BLOCK:])))
    return outs


def _dil_kernel(q_ref, k_ref, v_ref, o_ref, b1, b4, b16, s1, s4, s16):
    hp = pl.program_id(0)

    @pl.when(pl.program_id(1) == 0)
    def _():
        _dil_tables(hp, b1, b4, b16)

    head_a = lax.broadcasted_iota(jnp.int32, (1, 2 * SWA_DIM), 1) < SWA_DIM
    u = DIL_UNROLL
    nblk = q_ref.shape[2] * RES // SWA_BLOCK

    def cat(ref, pieces):
        return jnp.concatenate([ref[0, r, rows, :] for r, rows in pieces], axis=0)

    def body1(it, carry):
        tiles, where = [], []
        for j in range(u):
            n = it * u + j
            qrows = pl.ds(pl.multiple_of(n * 8, 8), 8)
            krows = pl.ds(pl.multiple_of(jnp.maximum(n * 8 - 8, 0), 8), 16)
            bias = b1[0] if j else jnp.where(n == 0, b1[1], b1[0])
            tiles.append((cat(q_ref, [(r, qrows) for r in range(RES)]),
                          cat(k_ref, [(r, krows) for r in range(RES)]),
                          cat(v_ref, [(r, krows) for r in range(RES)]), bias))
            where.append(qrows)
        for qrows, stats in zip(where, _attend(tiles, head_a)):
            for i, val in enumerate(stats):
                for r in range(RES):
                    s1[i, r, qrows, :] = val[8 * r:8 * r + 8]
        return carry

    lax.fori_loop(0, nblk // u, body1, 0)

    def body4(it, carry):
        tiles, where = [], []
        for j in range(u):
            r4, n = it * (u // 4) + j // 4, j % 4
            qrows = slice(32 * n, 32 * n + 32)
            k0 = max(32 * n - 32, 0)
            krows = slice(k0, k0 + 64)
            grp = [r4 + 4 * g for g in range(4)]
            tiles.append((cat(q_ref, [(r, qrows) for r in grp]), cat(k_ref, [(r, krows) for r in grp]),
                          cat(v_ref, [(r, krows) for r in grp]), b4[0] if n else b4[1]))
            where.append((grp, qrows))
        for (grp, qrows), stats in zip(where, _attend(tiles, head_a)):
            for i, val in enumerate(stats):
                for g, r in enumerate(grp):
                    s4[i, r, qrows, :] = val[32 * g:32 * g + 32]
        return carry

    lax.fori_loop(0, nblk // u, body4, 0)

    def body16(it, carry):
        rs = [it * u + j for j in range(u)]
        tiles = [(q_ref[0, r], k_ref[0, r], v_ref[0, r], b16[...]) for r in rs]
        for r, stats in zip(rs, _attend(tiles, head_a)):
            for i, val in enumerate(stats):
                s16[i, r] = val
        return carry

    lax.fori_loop(0, RES // u, body16, 0)

    m = jnp.maximum(jnp.maximum(s1[2], s4[2]), s16[2])
    ws = [jnp.exp2(s[2] - m) for s in (s1, s4, s16)]
    num = sum(w * s[0] for w, s in zip(ws, (s1, s4, s16)))
    den = sum(w * s[1] for w, s in zip(ws, (s1, s4, s16)))
    o_ref[0] = num / den


def _dilated(qkvb):
    batch, _, sub, _ = qkvb.shape
    npair = SWA_HEADS // 2
    spec = lambda off: pl.BlockSpec((1, RES, sub, 2 * SWA_DIM), lambda p, b: (b, 0, 0, off + p))
    wide = (2 * SWA_BLOCK, 2 * SWA_BLOCK)
    res = pltpu.VMEM((3, RES, sub, 2 * SWA_DIM), F32)
    return pl.pallas_call(
        _dil_kernel,
        grid=(npair, batch),
        in_specs=[spec(0), spec(npair), spec(2 * npair)],
        out_specs=spec(0),
        out_shape=jax.ShapeDtypeStruct((batch, RES, sub, SWA_W), F32),
        scratch_shapes=[pltpu.VMEM((2,) + wide, F32), pltpu.VMEM((2,) + wide, F32),
                        pltpu.VMEM((2 * SWA_BLOCK, SWA_BLOCK), F32), res, res, res],
        compiler_params=_params("arbitrary", "arbitrary"),
        name="dilated_attn",
    )(qkvb, qkvb, qkvb)


def _memkv_kernel(m_ref, g_ref, wk_ref, wv_ref, k_ref, v_ref):
    mn = _rms(m_ref[...], g_ref[...]).astype(BF16)
    k_ref[...] = _dot(mn, wk_ref[...].astype(BF16)).astype(BF16)
    v_ref[...] = _dot(mn, wv_ref[...].astype(BF16)).astype(BF16)


def _memkv(mem, g, wk, wv):
    t = mem.shape[0]
    tm = min(TM, t)
    row = pl.BlockSpec((tm, D_MODEL), lambda i: (i, 0))
    return pl.pallas_call(
        _memkv_kernel,
        grid=(t // tm,),
        in_specs=[row, _resident((1, D_MODEL)), _resident((D_MODEL, D_MODEL)),
                  _resident((D_MODEL, D_MODEL))],
        out_specs=[row, row],
        out_shape=[jax.ShapeDtypeStruct((t, D_MODEL), BF16)] * 2,
        compiler_params=_params("parallel"),
        name="mem_kv",
    )(mem, g, wk, wv)


def _outmem_kernel(x_ref, oa_ref, ob_ref, woa_ref, wob_ref, gmix_ref, gpre_ref, wq_ref,
                   k_ref, v_ref, wo_ref, gpost_ref, o_ref, att, obn):
    tm = x_ref.shape[0]
    rs = tm // OUT_SUB
    subs = [slice(i * rs, (i + 1) * rs) for i in range(OUT_SUB)]
    woa, wob, wq, wo = (w[...].astype(BF16) for w in (woa_ref, wob_ref, wq_ref, wo_ref))
    mixes = [_dot(oa_ref[s, :].astype(BF16), woa) for s in subs]
    for l in range(SWA_W // 128):
        for r in range(RES):
            obn[l, pl.ds(r, tm // RES, stride=RES), :] = ob_ref[0, r, :, 128 * l:128 * l + 128]
    xs = []
    for s, mix in zip(subs, mixes):
        ob = jnp.concatenate([obn[l, s, :] for l in range(SWA_W // 128)], axis=1).astype(BF16)
        xs.append(x_ref[s, :] + _rms(mix + _dot(ob, wob), gmix_ref[...]))
    qs = [(_dot(_rms(x, gpre_ref[...]).astype(BF16), wq) * (MEM_DIM ** -0.5)).astype(BF16)
          for x in xs]
    for h in range(MEM_HEADS):
        hl = slice(h * MEM_DIM, (h + 1) * MEM_DIM)
        scores = [_dot_nt(q[:, hl], k_ref[:, hl]) for q in qs]
        for s, sc in zip(subs, scores):
            e = jnp.exp(sc - jnp.max(sc, axis=-1, keepdims=True))
            p = e / jnp.sum(e, axis=-1, keepdims=True)
            att[s, hl] = _dot(p.astype(BF16), v_ref[:, hl]).astype(BF16)
    cs = [_dot(att[s, :], wo) for s in subs]
    for s, x, c in zip(subs, xs, cs):
        o_ref[s, :] = x + _rms(c, gpost_ref[...])


def _outmem(x, oa, ob, woa, wob, gmix, gpre, wq, kmem, vmem, wo, gpost, seq, n_mem):
    t = x.shape[0]
    per_seq = seq // TM_OUT
    row = lambda w: pl.BlockSpec((TM_OUT, w), lambda i: (i, 0))
    kv = pl.BlockSpec((n_mem, D_MODEL), lambda i: (i // per_seq, 0))
    return pl.pallas_call(
        _outmem_kernel,
        grid=(t // TM_OUT,),
        in_specs=[row(D_MODEL), row(GDN_W),
                  pl.BlockSpec((1, RES, TM_OUT // RES, SWA_W), lambda i: (i // per_seq, 0, i % per_seq, 0)),
                  pl.BlockSpec((GDN_W, D_MODEL), lambda i: (0, 0), pipeline_mode=pl.Buffered(1)),
                  pl.BlockSpec((SWA_W, D_MODEL), lambda i: (GDN_W // SWA_W, 0), pipeline_mode=pl.Buffered(1)),
                  _resident((1, D_MODEL)),
                  _resident((1, D_MODEL)), _resident((D_MODEL, D_MODEL)), kv, kv,
                  _resident((D_MODEL, D_MODEL)), _resident((1, D_MODEL))],
        out_specs=row(D_MODEL),
        out_shape=jax.ShapeDtypeStruct((t, D_MODEL), F32),
        scratch_shapes=[pltpu.VMEM((TM_OUT, D_MODEL), BF16), pltpu.VMEM((SWA_W // 128, TM_OUT, 128), F32)],
        compiler_params=_params("parallel"),
        name="out_mem_attn",
    )(x, oa, ob, woa, wob, gmix, gpre, wq, kmem, vmem, wo, gpost)


def _layer(x, mem, p, batch, seq, n_mem):
    bf = lambda w: w.astype(BF16)
    row = lambda g: g.reshape(1, -1)
    x = _ffn(x, row(p["ffn1_pre_g"]), p["ffn1_w_gate"], p["ffn1_w_up"],
             p["ffn1_w_down"], row(p["ffn1_post_g"]))

    w_in = p["w_in"]
    o_z, o_a, o_b = 3 * GDN_W, 4 * GDN_W, 4 * GDN_W + 2 * GDN_HEADS
    wab = jnp.pad(w_in[:, o_a:o_b], ((0, 0), (0, 128 - 2 * GDN_HEADS)))
    cw = jnp.pad(p["gdn_conv_w"], ((0, 8 - GDN_CONV), (0, 0)))
    qkva, z, abc, qkvb = _proj(x, row(p["mix_pre_g"]), bf(w_in[:, :o_z]), bf(w_in[:, o_z:o_a]),
                               bf(wab), bf(w_in[:, o_b:]), cw, batch, seq)

    pad_lane = lambda v: jnp.pad(v.reshape(1, -1), ((0, 0), (0, 128 - GDN_HEADS)))
    per_seq = lambda a: a.reshape(batch, seq, a.shape[-1])
    o_gdn = _gdn(per_seq(qkva), per_seq(z), per_seq(abc), pad_lane(p["gdn_a_log"]),
                 pad_lane(p["gdn_dt_bias"]), row(p["gdn_norm_g"])).reshape(batch * seq, GDN_W)
    o_dil = _dilated(qkvb)

    kmem, vmem = _memkv(mem, row(p["mem_kv_g"]), p["mem_wk"], p["mem_wv"])
    x = _outmem(x, o_gdn, o_dil, p["w_out"], p["w_out"], row(p["mix_post_g"]),
                row(p["mem_pre_g"]), p["mem_wq"], kmem, vmem, p["mem_wo"],
                row(p["mem_post_g"]), seq, n_mem)

    return _ffn(x, row(p["ffn2_pre_g"]), p["ffn2_w_gate"], p["ffn2_w_up"],
                p["ffn2_w_down"], row(p["ffn2_post_g"]))


_NAMES = ("ffn1_pre_g", "ffn1_w_gate", "ffn1_w_up", "ffn1_w_down", "ffn1_post_g",
          "mix_pre_g", "w_in", "gdn_conv_w", "gdn_a_log", "gdn_dt_bias", "gdn_norm_g", "w_out",
          "mix_post_g", "mem_pre_g", "mem_kv_g", "mem_wq", "mem_wk", "mem_wv", "mem_wo",
          "mem_post_g", "ffn2_pre_g", "ffn2_w_gate", "ffn2_w_up", "ffn2_w_down", "ffn2_post_g")


def kernel(x, mem, ffn1_pre_g, ffn1_w_gate, ffn1_w_up, ffn1_w_down, ffn1_post_g, mix_pre_g, w_in, gdn_conv_w, gdn_a_log, gdn_dt_bias, gdn_norm_g, w_out, mix_post_g, mem_pre_g, mem_kv_g, mem_wq, mem_wk, mem_wv, mem_wo, mem_post_g, ffn2_pre_g, ffn2_w_gate, ffn2_w_up, ffn2_w_down, ffn2_post_g):
    stacked = dict(zip(_NAMES, (ffn1_pre_g, ffn1_w_gate, ffn1_w_up, ffn1_w_down, ffn1_post_g,
                                mix_pre_g, w_in, gdn_conv_w, gdn_a_log, gdn_dt_bias, gdn_norm_g, w_out,
                                mix_post_g, mem_pre_g, mem_kv_g, mem_wq, mem_wk, mem_wv, mem_wo,
                                mem_post_g, ffn2_pre_g, ffn2_w_gate, ffn2_w_up, ffn2_w_down,
                                ffn2_post_g)))
    batch, seq, _ = x.shape
    n_mem = mem.shape[1]
    xf = x.reshape(batch * seq, D_MODEL)
    memf = mem.reshape(batch * n_mem, D_MODEL)
    for l in range(ffn1_pre_g.shape[0]):
        xf = _layer(xf, memf, {k: v[l] for k, v in stacked.items()}, batch, seq, n_mem)
    return xf.reshape(batch, seq, D_MODEL)
```

```python
import functools

import jax
import jax.numpy as jnp
from jax import lax
from jax.experimental import pallas as pl
from jax.experimental.pallas import tpu as pltpu

F32 = jnp.float32
BF16 = jnp.bfloat16

D_MODEL = 1024
D_FF = 2816
EPS = 1e-6
GDN_HEADS = 4
GDN_DIM = 128
GDN_W = GDN_HEADS * GDN_DIM
GDN_CONV = 4
CHUNK = 64
SWA_HEADS = 8
SWA_DIM = 64
SWA_W = SWA_HEADS * SWA_DIM
SWA_BLOCK = 128
DILATIONS = (1, 4, 16)
RES = 16
MEM_HEADS = 4
MEM_DIM = D_MODEL // MEM_HEADS
NEG = -1e30
LOG2E = 1.4426950408889634

V7X_VMEM_BYTES = 64 * 1024 * 1024
VMEM_LIMIT = V7X_VMEM_BYTES - 8 * 1024 * 1024

TM = 512
HALO = 16
TM_FFN = 512
FFN_FC = 256
PROJ_TILE = 256
TM_OUT = 1024
OUT_SUB = 4
GDN_GB = 8
DIL_UNROLL = 16
DIL_SKEW = 3


def _rms(x, g):
    return x * lax.rsqrt(jnp.mean(x * x, axis=-1, keepdims=True) + EPS) * g


def _dot(a, b):
    return jnp.dot(a, b, preferred_element_type=F32)


def _dot_nt(a, b):
    return lax.dot_general(a, b, (((1,), (1,)), ((), ())), preferred_element_type=F32)


def _dot_tn(a, b):
    return lax.dot_general(a, b, (((0,), (0,)), ((), ())), preferred_element_type=F32)


def _resident(shape):
    return pl.BlockSpec(shape, lambda *_: (0,) * len(shape), pipeline_mode=pl.Buffered(1))


def _params(*sem):
    return pltpu.CompilerParams(dimension_semantics=sem, vmem_limit_bytes=VMEM_LIMIT)


def _ffn_kernel(x_ref, gpre_ref, wg_ref, wu_ref, wd_ref, gpost_ref, o_ref, a_ref):
    x = x_ref[...]
    xn = _rms(x, gpre_ref[...]).astype(BF16)
    for c in range(D_FF // FFN_FC):
        sl = slice(c * FFN_FC, (c + 1) * FFN_FC)
        g = _dot(xn, wg_ref[:, sl].astype(BF16))
        u = _dot(xn, wu_ref[:, sl].astype(BF16))
        a_ref[:, sl] = (g * jax.nn.sigmoid(g) * u).astype(BF16)
    f = _dot(a_ref[...], wd_ref[...].astype(BF16))
    o_ref[...] = x + 0.5 * _rms(f, gpost_ref[...])


def _ffn(x, gpre, wg, wu, wd, gpost):
    t = x.shape[0]
    row = pl.BlockSpec((TM_FFN, D_MODEL), lambda i: (i, 0))
    return pl.pallas_call(
        _ffn_kernel,
        grid=(t // TM_FFN,),
        in_specs=[row, _resident((1, D_MODEL)), _resident((D_MODEL, D_FF)),
                  _resident((D_MODEL, D_FF)), _resident((D_FF, D_MODEL)),
                  _resident((1, D_MODEL))],
        out_specs=row,
        out_shape=jax.ShapeDtypeStruct((t, D_MODEL), F32),
        scratch_shapes=[pltpu.VMEM((TM_FFN, D_FF), BF16)],
        compiler_params=_params("parallel"),
        name="ffn",
    )(x, gpre, wg, wu, wd, gpost)


def _proj_kernel(x_ref, xh_ref, g_ref, wa_ref, wz_ref, wab_ref, wb_ref, cw_ref,
                 qkva_ref, z_ref, abc_ref, qkvb_ref, rb, *, per_seq):
    g = g_ref[...]
    h = _rms(x_ref[...], g).astype(BF16)
    keep = jnp.where(pl.program_id(0) % per_seq == 0, 0.0, 1.0)
    hx = jnp.concatenate([(_rms(xh_ref[...], g) * keep).astype(BF16), h], axis=0)

    def conv_tile(t, xa):
        for hd in range(PROJ_TILE // GDN_DIM):
            lanes = slice(PROJ_TILE * t + hd * GDN_DIM, PROJ_TILE * t + (hd + 1) * GDN_DIM)
            xh = xa[:, hd * GDN_DIM:(hd + 1) * GDN_DIM]
            y = cw_ref[0:1, lanes] * xh
            for j in range(1, GDN_CONV):
                y = pltpu.roll(y, 1, axis=0) + cw_ref[j:j + 1, lanes] * xh
            y = y[HALO:]
            y = y * jax.nn.sigmoid(y)
            if lanes.start < 2 * GDN_W:
                scale = GDN_DIM ** -0.5 if lanes.start < GDN_W else 1.0
                y = y * (lax.rsqrt(jnp.sum(y * y, axis=-1, keepdims=True) + EPS) * scale)
            qkva_ref[:, lanes] = y

    def attn_tile(t):
        yb = _dot(h, wb_ref[:, PROJ_TILE * t:PROJ_TILE * (t + 1)])
        for half in range(PROJ_TILE // 128):
            l = t * (PROJ_TILE // 128) + half
            rb[l] = yb[:, 128 * half:128 * half + 128]
            for r in range(RES):
                qkvb_ref[0, r, :, 128 * l:128 * l + 128] = rb[l, pl.ds(r, TM // RES, stride=RES), :]

    n_a = 3 * GDN_W // PROJ_TILE
    prev = None
    for t in range(n_a):
        xa = _dot(hx, wa_ref[:, PROJ_TILE * t:PROJ_TILE * (t + 1)])
        attn_tile(t)
        if t:
            conv_tile(t - 1, prev)
        prev = xa
    z_ref[...] = _dot(h, wz_ref[...])
    abc_ref[...] = _dot(h, wab_ref[...])
    conv_tile(n_a - 1, prev)


def _proj(x, g, wa, wz, wab, wb, cw, batch, seq):
    t = x.shape[0]
    per_seq = seq // TM
    row = lambda w: pl.BlockSpec((TM, w), lambda i: (i, 0))
    halo = pl.BlockSpec((HALO, D_MODEL), lambda i: (jnp.maximum(i * (TM // HALO) - 1, 0), 0))
    return pl.pallas_call(
        functools.partial(_proj_kernel, per_seq=per_seq),
        grid=(t // TM,),
        in_specs=[row(D_MODEL), halo, _resident((1, D_MODEL)), _resident(wa.shape),
                  _resident(wz.shape), _resident(wab.shape), _resident(wb.shape),
                  _resident(cw.shape)],
        out_specs=[row(3 * GDN_W), row(GDN_W), row(128),
                   pl.BlockSpec((1, RES, TM // RES, 3 * SWA_W),
                                lambda i: (i // per_seq, 0, i % per_seq, 0))],
        out_shape=[jax.ShapeDtypeStruct((t, 3 * GDN_W), F32),
                   jax.ShapeDtypeStruct((t, GDN_W), F32),
                   jax.ShapeDtypeStruct((t, 128), F32),
                   jax.ShapeDtypeStruct((batch, RES, seq // RES, 3 * SWA_W), F32)],
        scratch_shapes=[pltpu.VMEM((3 * SWA_W // 128, TM, 128), F32)],
        compiler_params=_params("parallel"),
        name="mixer_proj",
    )(x, x, g, wa, wz, wab, wb, cw)


def _softplus(x):
    return jnp.maximum(x, 0.0) + jnp.log(1.0 + jnp.exp(-jnp.abs(x)))


def _level_masks():
    row = lax.broadcasted_iota(jnp.int32, (CHUNK, CHUNK), 0)
    col = lax.broadcasted_iota(jnp.int32, (CHUNK, CHUNK), 1)
    sels = []
    for lb in range(CHUNK.bit_length() - 1):
        sels.append(((row >> (lb + 1)) == (col >> (lb + 1))) & (((row >> lb) & 1) == 1)
                    & (((col >> lb) & 1) == 0))
    return row, col, sels


def _unit_lower_inverse(a_list, eye, sels):
    ts = [eye - jnp.where(sels[0], a, 0.0) for a in a_list]
    for sel in sels[1:]:
        tb = [t.astype(BF16) for t in ts]
        ps = [_dot(jnp.where(sel, a, 0.0).astype(BF16), t) for a, t in zip(a_list, tb)]
        ts = [t - _dot(t16, p.astype(BF16)) for t, t16, p in zip(ts, tb, ps)]
    return ts


def _gdn_kernel(x_ref, z_ref, abc_ref, alc_ref, dtc_ref, ng_ref, o_ref, st):
    nbatch = x_ref.shape[0]

    @pl.when(pl.program_id(0) == 0)
    def _():
        st[...] = jnp.zeros_like(st)

    row, col, sels = _level_masks()
    tril = col <= row
    strict = col < row
    eye = jnp.where(row == col, 1.0, 0.0).astype(F32)
    lower = jnp.where(tril, 1.0, 0.0).astype(F32)
    upper = jnp.where(row <= col, 1.0, 0.0).astype(F32)
    hi = lax.Precision.HIGHEST

    def group(gi, carry):
        items = []
        for bb in range(GDN_GB):
            b = gi * GDN_GB + bb
            abc = abc_ref[b]
            g_col = -jnp.exp(alc_ref[...]) * _softplus(abc + dtc_ref[...])
            gc_col = jnp.dot(lower, g_col, precision=hi, preferred_element_type=F32)
            gc_row = lax.dot_general(g_col, upper, (((0,), (0,)), ((), ())), precision=hi,
                                     preferred_element_type=F32)
            beta_all = jax.nn.sigmoid(abc)
            for h in range(GDN_HEADS):
                items.append(dict(
                    b=b, h=h,
                    q=x_ref[b, :, h * GDN_DIM:(h + 1) * GDN_DIM],
                    k=x_ref[b, :, GDN_W + h * GDN_DIM:GDN_W + (h + 1) * GDN_DIM],
                    v=x_ref[b, :, 2 * GDN_W + h * GDN_DIM:2 * GDN_W + (h + 1) * GDN_DIM],
                    gcc=gc_col[:, h:h + 1], gcr=gc_row[h:h + 1, :],
                    gl=gc_col[CHUNK - 1:CHUNK, h:h + 1], beta=beta_all[:, 4 + h:5 + h]))

        kqs = []
        for it in items:
            it["kb"] = it["k"] * it["beta"]
            kqs.append(_dot_nt(jnp.concatenate([it["kb"], it["q"]], axis=0).astype(BF16),
                               it["k"].astype(BF16)))
        a_list = []
        for it, kq in zip(items, kqs):
            decay = jnp.exp(jnp.where(tril, it["gcc"] - it["gcr"], NEG))
            a_list.append(jnp.where(strict, kq[:CHUNK] * decay, 0.0))
            it["qk"] = jnp.where(tril, kq[CHUNK:] * decay, 0.0).astype(BF16)
        ts = _unit_lower_inverse(a_list, eye, sels)
        uws = []
        for it, t in zip(items, ts):
            eg = jnp.exp(it["gcc"])
            rhs = jnp.concatenate([it["v"] * it["beta"], it["kb"] * eg], axis=1).astype(BF16)
            uws.append(_dot(t.astype(BF16), rhs))
            it["qd"] = (it["q"] * eg).astype(BF16)
            it["kd"] = (it["k"] * jnp.exp(it["gl"] - it["gcc"])).astype(BF16)

        states, wss = [], []
        for it, uw in zip(items, uws):
            s = st[it["b"] * GDN_HEADS + it["h"]]
            states.append(s)
            lhs = jnp.concatenate([uw[:, GDN_DIM:].astype(BF16), it["qd"]], axis=0)
            wss.append(_dot(lhs, s.astype(BF16)))
        for it, uw, s, ws in zip(items, uws, states, wss):
            vb = (uw[:, :GDN_DIM] - ws[:CHUNK]).astype(BF16)
            o = ws[CHUNK:] + _dot(it["qk"], vb)
            st[it["b"] * GDN_HEADS + it["h"]] = s * jnp.exp(it["gl"]) + _dot_tn(it["kd"], vb)
            hl = slice(it["h"] * GDN_DIM, (it["h"] + 1) * GDN_DIM)
            zz = z_ref[it["b"], :, hl]
            o = o * lax.rsqrt(jnp.mean(o * o, axis=-1, keepdims=True) + EPS)
            o_ref[it["b"], :, hl] = o * ng_ref[...] * (zz * jax.nn.sigmoid(zz))
        return carry

    lax.fori_loop(0, nbatch // GDN_GB, group, 0)


def _gdn(qkva, z, abc, alc, dtc, ng):
    batch, seq, _ = qkva.shape
    blk = lambda w: pl.BlockSpec((batch, CHUNK, w), lambda c: (0, c, 0))
    const = lambda shape: pl.BlockSpec(shape, lambda c: (0, 0))
    return pl.pallas_call(
        _gdn_kernel,
        grid=(seq // CHUNK,),
        in_specs=[blk(3 * GDN_W), blk(GDN_W), blk(128),
                  const((1, 128)), const((1, 128)), const((1, GDN_DIM))],
        out_specs=blk(GDN_W),
        out_shape=jax.ShapeDtypeStruct((batch, seq, GDN_W), F32),
        scratch_shapes=[pltpu.VMEM((batch * GDN_HEADS, GDN_DIM, GDN_DIM), F32)],
        compiler_params=_params("arbitrary"),
        name="gdn",
    )(qkva, z, abc, alc, dtc, ng)


def _dil_bias(delta, dilation, slopes):
    valid = (delta >= 0) & (delta <= SWA_BLOCK)
    df = delta.astype(F32) * LOG2E
    return jnp.concatenate([jnp.where(valid, -(s * float(dilation)) * df, NEG) for s in slopes], axis=0)


def _dil_tables(hp, b1, b4, b16):
    def iota(shape, dim):
        return lax.broadcasted_iota(jnp.int32, shape, dim)

    def slope(head, shape):
        e = jnp.full(shape, 126, jnp.int32) - head
        return lax.bitcast_convert_type(lax.shift_left(e, jnp.full_like(e, 23)), F32)

    wide = (SWA_BLOCK, 2 * SWA_BLOCK)
    sl = [slope(2 * hp, wide), slope(2 * hp + 1, wide)]
    p, c = iota(wide, 0), iota(wide, 1)
    d = 16 * ((p & 7) - (c & 15)) + ((p >> 3) - (c >> 4))
    b1[0] = _dil_bias(d + 16 * 8, 1, sl)
    b1[1] = _dil_bias(d, 1, sl)
    d = 4 * ((p & 31) - (c & 63)) + ((p >> 5) - (c >> 6))
    b4[0] = _dil_bias(d + 4 * 32, 4, sl)
    b4[1] = _dil_bias(d, 4, sl)
    sq = (SWA_BLOCK, SWA_BLOCK)
    b16[...] = _dil_bias(iota(sq, 0) - iota(sq, 1), 16, [slope(2 * hp, sq), slope(2 * hp + 1, sq)])


def _attend(tiles, head_a):
    def score(tile):
        q, k, v, bias = tile
        qa = jnp.where(head_a, q, 0.0)
        qs = (jnp.concatenate([qa, q - qa], axis=0) * (SWA_DIM ** -0.5 * LOG2E)).astype(BF16)
        return _dot_nt(qs, k.astype(BF16)) + bias

    def prob(s):
        m = jnp.max(s, axis=-1, keepdims=True)
        return jnp.exp2(s - m).astype(BF16), m

    def out(tile, e, m):
        v = tile[2]
        v1 = jnp.concatenate([v.astype(BF16), jnp.ones(v.shape, BF16)], axis=1)
        ol = _dot(e, v1)
        top, bot = ol[:SWA_BLOCK], ol[SWA_BLOCK:]
        w = 2 * SWA_DIM
        return (jnp.where(head_a, top[:, :w], bot[:, :w]), jnp.where(head_a, top[:, w:], bot[:, w:]),
                jnp.where(head_a, m[:SWA_BLOCK], m[SWA_BLOCK:]))

    n = len(tiles)
    scores, probs, outs = {}, {}, []
    for j in range(n + 2 * DIL_SKEW):
        if j < n:
            scores[j] = score(tiles[j])
        if 0 <= j - DIL_SKEW < n:
            probs[j - DIL_SKEW] = prob(scores.pop(j - DIL_SKEW))
        if j - 2 * DIL_SKEW >= 0:
            outs.append(out(tiles[j - 2 * DIL_SKEW], *probs.pop(j - 2 * DIL_SKEW)))
    return outs


def _dil_kernel(q_ref, k_ref, v_ref, o_ref, b1, b4, b16, s1, s4, s16):
    hp = pl.program_id(0)

    @pl.when(pl.program_id(1) == 0)
    def _():
        _dil_tables(hp, b1, b4, b16)

    head_a = lax.broadcasted_iota(jnp.int32, (1, 2 * SWA_DIM), 1) < SWA_DIM
    u = DIL_UNROLL
    nblk = q_ref.shape[2] * RES // SWA_BLOCK

    def cat(ref, pieces):
        return jnp.concatenate([ref[0, r, rows, :] for r, rows in pieces], axis=0)

    def body1(it, carry):
        tiles, where = [], []
        for j in range(u):
            n = it * u + j
            qrows = pl.ds(pl.multiple_of(n * 8, 8), 8)
            krows = pl.ds(pl.multiple_of(jnp.maximum(n * 8 - 8, 0), 8), 16)
            bias = b1[0] if j else jnp.where(n == 0, b1[1], b1[0])
            tiles.append((cat(q_ref, [(r, qrows) for r in range(RES)]),
                          cat(k_ref, [(r, krows) for r in range(RES)]),
                          cat(v_ref, [(r, krows) for r in range(RES)]), bias))
            where.append(qrows)
        for qrows, stats in zip(where, _attend(tiles, head_a)):
            for i, val in enumerate(stats):
                for r in range(RES):
                    s1[i, r, qrows, :] = val[8 * r:8 * r + 8]
        return carry

    lax.fori_loop(0, nblk // u, body1, 0)

    def body4(it, carry):
        tiles, where = [], []
        for j in range(u):
            r4, n = it * (u // 4) + j // 4, j % 4
            qrows = slice(32 * n, 32 * n + 32)
            k0 = max(32 * n - 32, 0)
            krows = slice(k0, k0 + 64)
            grp = [r4 + 4 * g for g in range(4)]
            tiles.append((cat(q_ref, [(r, qrows) for r in grp]), cat(k_ref, [(r, krows) for r in grp]),
                          cat(v_ref, [(r, krows) for r in grp]), b4[0] if n else b4[1]))
            where.append((grp, qrows))
        for (grp, qrows), stats in zip(where, _attend(tiles, head_a)):
            for i, val in enumerate(stats):
                for g, r in enumerate(grp):
                    s4[i, r, qrows, :] = val[32 * g:32 * g + 32]
        return carry

    lax.fori_loop(0, nblk // u, body4, 0)

    def body16(it, carry):
        rs = [it * u + j for j in range(u)]
        tiles = [(q_ref[0, r], k_ref[0, r], v_ref[0, r], b16[...]) for r in rs]
        for r, stats in zip(rs, _attend(tiles, head_a)):
            for i, val in enumerate(stats):
                s16[i, r] = val
        return carry

    lax.fori_loop(0, RES // u, body16, 0)

    m = jnp.maximum(jnp.maximum(s1[2], s4[2]), s16[2])
    ws = [jnp.exp2(s[2] - m) for s in (s1, s4, s16)]
    num = sum(w * s[0] for w, s in zip(ws, (s1, s4, s16)))
    den = sum(w * s[1] for w, s in zip(ws, (s1, s4, s16)))
    o_ref[0] = num / den


def _dilated(qkvb):
    batch, _, sub, _ = qkvb.shape
    npair = SWA_HEADS // 2
    spec = lambda off: pl.BlockSpec((1, RES, sub, 2 * SWA_DIM), lambda p, b: (b, 0, 0, off + p))
    wide = (2 * SWA_BLOCK, 2 * SWA_BLOCK)
    res = pltpu.VMEM((3, RES, sub, 2 * SWA_DIM), F32)
    return pl.pallas_call(
        _dil_kernel,
        grid=(npair, batch),
        in_specs=[spec(0), spec(npair), spec(2 * npair)],
        out_specs=spec(0),
        out_shape=jax.ShapeDtypeStruct((batch, RES, sub, SWA_W), F32),
        scratch_shapes=[pltpu.VMEM((2,) + wide, F32), pltpu.VMEM((2,) + wide, F32),
                        pltpu.VMEM((2 * SWA_BLOCK, SWA_BLOCK), F32), res, res, res],
        compiler_params=_params("arbitrary", "arbitrary"),
        name="dilated_attn",
    )(qkvb, qkvb, qkvb)


def _memkv_kernel(m_ref, g_ref, wk_ref, wv_ref, k_ref, v_ref):
    mn = _rms(m_ref[...], g_ref[...]).astype(BF16)
    k_ref[...] = _dot(mn, wk_ref[...].astype(BF16)).astype(BF16)
    v_ref[...] = _dot(mn, wv_ref[...].astype(BF16)).astype(BF16)


def _memkv(mem, g, wk, wv):
    t = mem.shape[0]
    tm = min(TM, t)
    row = pl.BlockSpec((tm, D_MODEL), lambda i: (i, 0))
    return pl.pallas_call(
        _memkv_kernel,
        grid=(t // tm,),
        in_specs=[row, _resident((1, D_MODEL)), _resident((D_MODEL, D_MODEL)),
                  _resident((D_MODEL, D_MODEL))],
        out_specs=[row, row],
        out_shape=[jax.ShapeDtypeStruct((t, D_MODEL), BF16)] * 2,
        compiler_params=_params("parallel"),
        name="mem_kv",
    )(mem, g, wk, wv)


def _outmem_kernel(x_ref, oa_ref, ob_ref, woa_ref, wob_ref, gmix_ref, gpre_ref, wq_ref,
                   k_ref, v_ref, wo_ref, gpost_ref, o_ref, att, obn):
    tm = x_ref.shape[0]
    rs = tm // OUT_SUB
    subs = [slice(i * rs, (i + 1) * rs) for i in range(OUT_SUB)]
    woa, wob, wq, wo = (w[...].astype(BF16) for w in (woa_ref, wob_ref, wq_ref, wo_ref))
    mixes = [_dot(oa_ref[s, :].astype(BF16), woa) for s in subs]
    for l in range(SWA_W // 128):
        for r in range(RES):
            obn[l, pl.ds(r, tm // RES, stride=RES), :] = ob_ref[0, r, :, 128 * l:128 * l + 128]
    xs = []
    for s, mix in zip(subs, mixes):
        ob = jnp.concatenate([obn[l, s, :] for l in range(SWA_W // 128)], axis=1).astype(BF16)
        xs.append(x_ref[s, :] + _rms(mix + _dot(ob, wob), gmix_ref[...]))
    qs = [(_dot(_rms(x, gpre_ref[...]).astype(BF16), wq) * (MEM_DIM ** -0.5)).astype(BF16)
          for x in xs]
    for h in range(MEM_HEADS):
        hl = slice(h * MEM_DIM, (h + 1) * MEM_DIM)
        scores = [_dot_nt(q[:, hl], k_ref[:, hl]) for q in qs]
        for s, sc in zip(subs, scores):
            e = jnp.exp(sc - jnp.max(sc, axis=-1, keepdims=True))
            p = e / jnp.sum(e, axis=-1, keepdims=True)
            att[s, hl] = _dot(p.astype(BF16), v_ref[:, hl]).astype(BF16)
    cs = [_dot(att[s, :], wo) for s in subs]
    for s, x, c in zip(subs, xs, cs):
        o_ref[s, :] = x + _rms(c, gpost_ref[...])


def _outmem(x, oa, ob, woa, wob, gmix, gpre, wq, kmem, vmem, wo, gpost, seq, n_mem):
    t = x.shape[0]
    per_seq = seq // TM_OUT
    row = lambda w: pl.BlockSpec((TM_OUT, w), lambda i: (i, 0))
    kv = pl.BlockSpec((n_mem, D_MODEL), lambda i: (i // per_seq, 0))
    return pl.pallas_call(
        _outmem_kernel,
        grid=(t // TM_OUT,),
        in_specs=[row(D_MODEL), row(GDN_W),
                  pl.BlockSpec((1, RES, TM_OUT // RES, SWA_W), lambda i: (i // per_seq, 0, i % per_seq, 0)),
                  pl.BlockSpec((GDN_W, D_MODEL), lambda i: (0, 0), pipeline_mode=pl.Buffered(1)),
                  pl.BlockSpec((SWA_W, D_MODEL), lambda i: (GDN_W // SWA_W, 0), pipeline_mode=pl.Buffered(1)),
                  _resident((1, D_MODEL)),
                  _resident((1, D_MODEL)), _resident((D_MODEL, D_MODEL)), kv, kv,
                  _resident((D_MODEL, D_MODEL)), _resident((1, D_MODEL))],
        out_specs=row(D_MODEL),
        out_shape=jax.ShapeDtypeStruct((t, D_MODEL), F32),
        scratch_shapes=[pltpu.VMEM((TM_OUT, D_MODEL), BF16), pltpu.VMEM((SWA_W // 128, TM_OUT, 128), F32)],
        compiler_params=_params("parallel"),
        name="out_mem_attn",
    )(x, oa, ob, woa, wob, gmix, gpre, wq, kmem, vmem, wo, gpost)


def _layer(x, mem, p, batch, seq, n_mem):
    bf = lambda w: w.astype(BF16)
    row = lambda g: g.reshape(1, -1)
    x = _ffn(x, row(p["ffn1_pre_g"]), p["ffn1_w_gate"], p["ffn1_w_up"],
             p["ffn1_w_down"], row(p["ffn1_post_g"]))

    w_in = p["w_in"]
    o_z, o_a, o_b = 3 * GDN_W, 4 * GDN_W, 4 * GDN_W + 2 * GDN_HEADS
    wab = jnp.pad(w_in[:, o_a:o_b], ((0, 0), (0, 128 - 2 * GDN_HEADS)))
    cw = jnp.pad(p["gdn_conv_w"], ((0, 8 - GDN_CONV), (0, 0)))
    qkva, z, abc, qkvb = _proj(x, row(p["mix_pre_g"]), bf(w_in[:, :o_z]), bf(w_in[:, o_z:o_a]),
                               bf(wab), bf(w_in[:, o_b:]), cw, batch, seq)

    pad_lane = lambda v: jnp.pad(v.reshape(1, -1), ((0, 0), (0, 128 - GDN_HEADS)))
    per_seq = lambda a: a.reshape(batch, seq, a.shape[-1])
    o_gdn = _gdn(per_seq(qkva), per_seq(z), per_seq(abc), pad_lane(p["gdn_a_log"]),
                 pad_lane(p["gdn_dt_bias"]), row(p["gdn_norm_g"])).reshape(batch * seq, GDN_W)
    o_dil = _dilated(qkvb)

    kmem, vmem = _memkv(mem, row(p["mem_kv_g"]), p["mem_wk"], p["mem_wv"])
    x = _outmem(x, o_gdn, o_dil, p["w_out"], p["w_out"], row(p["mix_post_g"]),
                row(p["mem_pre_g"]), p["mem_wq"], kmem, vmem, p["mem_wo"],
                row(p["mem_post_g"]), seq, n_mem)

    return _ffn(x, row(p["ffn2_pre_g"]), p["ffn2_w_gate"], p["ffn2_w_up"],
                p["ffn2_w_down"], row(p["ffn2_post_g"]))


_NAMES = ("ffn1_pre_g", "ffn1_w_gate", "ffn1_w_up", "ffn1_w_down", "ffn1_post_g",
          "mix_pre_g", "w_in", "gdn_conv_w", "gdn_a_log", "gdn_dt_bias", "gdn_norm_g", "w_out",
          "mix_post_g", "mem_pre_g", "mem_kv_g", "mem_wq", "mem_wk", "mem_wv", "mem_wo",
          "mem_post_g", "ffn2_pre_g", "ffn2_w_gate", "ffn2_w_up", "ffn2_w_down", "ffn2_post_g")


def kernel(x, mem, ffn1_pre_g, ffn1_w_gate, ffn1_w_up, ffn1_w_down, ffn1_post_g, mix_pre_g, w_in, gdn_conv_w, gdn_a_log, gdn_dt_bias, gdn_norm_g, w_out, mix_post_g, mem_pre_g, mem_kv_g, mem_wq, mem_wk, mem_wv, mem_wo, mem_post_g, ffn2_pre_g, ffn2_w_gate, ffn2_w_up, ffn2_w_down, ffn2_post_g):
    stacked = dict(zip(_NAMES, (ffn1_pre_g, ffn1_w_gate, ffn1_w_up, ffn1_w_down, ffn1_post_g,
                                mix_pre_g, w_in, gdn_conv_w, gdn_a_log, gdn_dt_bias, gdn_norm_g, w_out,
                                mix_post_g, mem_pre_g, mem_kv_g, mem_wq, mem_wk, mem_wv, mem_wo,
                                mem_post_g, ffn2_pre_g, ffn2_w_gate, ffn2_w_up, ffn2_w_down,
                                ffn2_post_g)))
    batch, seq, _ = x.shape
    n_mem = mem.shape[1]
    xf = x.reshape(batch * seq, D_MODEL)
    memf = mem.reshape(batch * n_mem, D_MODEL)
    for l in range(ffn1_pre_g.shape[0]):
        xf = _layer(xf, memf, {k: v[l] for k, v in stacked.items()}, batch, seq, n_mem)
    return xf.reshape(batch, seq, D_MODEL)
```

```python
import functools

import jax
import jax.numpy as jnp
from jax import lax
from jax.experimental import pallas as pl
from jax.experimental.pallas import tpu as pltpu

F32 = jnp.float32
BF16 = jnp.bfloat16

D_MODEL = 1024
D_FF = 2816
EPS = 1e-6
GDN_HEADS = 4
GDN_DIM = 128
GDN_W = GDN_HEADS * GDN_DIM
GDN_CONV = 4
CHUNK = 64
SWA_HEADS = 8
SWA_DIM = 64
SWA_W = SWA_HEADS * SWA_DIM
SWA_BLOCK = 128
DILATIONS = (1, 4, 16)
RES = 16
MEM_HEADS = 4
MEM_DIM = D_MODEL // MEM_HEADS
NEG = -1e30
LOG2E = 1.4426950408889634

V7X_VMEM_BYTES = 64 * 1024 * 1024
VMEM_LIMIT = V7X_VMEM_BYTES - 8 * 1024 * 1024

TM = 512
HALO = 16
TM_FFN = 512
FFN_FC = 256
PROJ_TILE = 256
TM_OUT = 1024
OUT_SUB = 4
GDN_GROUPS = 1
DIL_UNROLL = 16
DIL_SKEW = 3


def _rms(x, g):
    return x * lax.rsqrt(jnp.mean(x * x, axis=-1, keepdims=True) + EPS) * g


def _dot(a, b):
    return jnp.dot(a, b, preferred_element_type=F32)


def _dot_nt(a, b):
    return lax.dot_general(a, b, (((1,), (1,)), ((), ())), preferred_element_type=F32)


def _dot_tn(a, b):
    return lax.dot_general(a, b, (((0,), (0,)), ((), ())), preferred_element_type=F32)


def _resident(shape):
    return pl.BlockSpec(shape, lambda *_: (0,) * len(shape), pipeline_mode=pl.Buffered(1))


def _params(*sem):
    return pltpu.CompilerParams(dimension_semantics=sem, vmem_limit_bytes=VMEM_LIMIT)


def _ffn_kernel(x_ref, gpre_ref, wg_ref, wu_ref, wd_ref, gpost_ref, o_ref, a_ref):
    x = x_ref[...]
    xn = _rms(x, gpre_ref[...]).astype(BF16)
    for c in range(D_FF // FFN_FC):
        sl = slice(c * FFN_FC, (c + 1) * FFN_FC)
        g = _dot(xn, wg_ref[:, sl].astype(BF16))
        u = _dot(xn, wu_ref[:, sl].astype(BF16))
        a_ref[:, sl] = (g * jax.nn.sigmoid(g) * u).astype(BF16)
    f = _dot(a_ref[...], wd_ref[...].astype(BF16))
    o_ref[...] = x + 0.5 * _rms(f, gpost_ref[...])


def _ffn(x, gpre, wg, wu, wd, gpost):
    t = x.shape[0]
    row = pl.BlockSpec((TM_FFN, D_MODEL), lambda i: (i, 0))
    return pl.pallas_call(
        _ffn_kernel,
        grid=(t // TM_FFN,),
        in_specs=[row, _resident((1, D_MODEL)), _resident((D_MODEL, D_FF)),
                  _resident((D_MODEL, D_FF)), _resident((D_FF, D_MODEL)),
                  _resident((1, D_MODEL))],
        out_specs=row,
        out_shape=jax.ShapeDtypeStruct((t, D_MODEL), F32),
        scratch_shapes=[pltpu.VMEM((TM_FFN, D_FF), BF16)],
        compiler_params=_params("parallel"),
        name="ffn",
    )(x, gpre, wg, wu, wd, gpost)


def _proj_kernel(x_ref, xh_ref, g_ref, wa_ref, wz_ref, wab_ref, wb_ref, cw_ref,
                 qkva_ref, z_ref, abc_ref, qkvb_ref, rb, *, per_seq):
    g = g_ref[...]
    h = _rms(x_ref[...], g).astype(BF16)
    keep = jnp.where(pl.program_id(0) % per_seq == 0, 0.0, 1.0)
    hx = jnp.concatenate([(_rms(xh_ref[...], g) * keep).astype(BF16), h], axis=0)

    def conv_tile(t, xa):
        for hd in range(PROJ_TILE // GDN_DIM):
            lanes = slice(PROJ_TILE * t + hd * GDN_DIM, PROJ_TILE * t + (hd + 1) * GDN_DIM)
            xh = xa[:, hd * GDN_DIM:(hd + 1) * GDN_DIM]
            y = cw_ref[0:1, lanes] * xh
            for j in range(1, GDN_CONV):
                y = pltpu.roll(y, 1, axis=0) + cw_ref[j:j + 1, lanes] * xh
            y = y[HALO:]
            y = y * jax.nn.sigmoid(y)
            if lanes.start < 2 * GDN_W:
                scale = GDN_DIM ** -0.5 if lanes.start < GDN_W else 1.0
                y = y * (lax.rsqrt(jnp.sum(y * y, axis=-1, keepdims=True) + EPS) * scale)
            qkva_ref[:, lanes] = y

    def attn_tile(t):
        yb = _dot(h, wb_ref[:, PROJ_TILE * t:PROJ_TILE * (t + 1)])
        for half in range(PROJ_TILE // 128):
            l = t * (PROJ_TILE // 128) + half
            rb[l] = yb[:, 128 * half:128 * half + 128]
            for r in range(RES):
                qkvb_ref[0, r, :, 128 * l:128 * l + 128] = rb[l, pl.ds(r, TM // RES, stride=RES), :]

    n_a = 3 * GDN_W // PROJ_TILE
    prev = None
    for t in range(n_a):
        xa = _dot(hx, wa_ref[:, PROJ_TILE * t:PROJ_TILE * (t + 1)])
        attn_tile(t)
        if t:
            conv_tile(t - 1, prev)
        prev = xa
    z_ref[...] = _dot(h, wz_ref[...])
    abc_ref[...] = _dot(h, wab_ref[...])
    conv_tile(n_a - 1, prev)


def _proj(x, g, w_in, wb, cw, batch, seq):
    t = x.shape[0]
    per_seq = seq // TM
    row = lambda w: pl.BlockSpec((TM, w), lambda i: (i, 0))
    halo = pl.BlockSpec((HALO, D_MODEL), lambda i: (jnp.maximum(i * (TM // HALO) - 1, 0), 0))
    col = lambda w, j: pl.BlockSpec((D_MODEL, w), lambda i: (0, j), pipeline_mode=pl.Buffered(1))
    return pl.pallas_call(
        functools.partial(_proj_kernel, per_seq=per_seq),
        grid=(t // TM,),
        in_specs=[row(D_MODEL), halo, _resident((1, D_MODEL)),
                  col(3 * GDN_W, 0), col(GDN_W, 3), col(128, 4 * GDN_W // 128),
                  _resident(wb.shape), _resident(cw.shape)],
        out_specs=[row(3 * GDN_W), row(GDN_W), row(128),
                   pl.BlockSpec((1, RES, TM // RES, 3 * SWA_W),
                                lambda i: (i // per_seq, 0, i % per_seq, 0))],
        out_shape=[jax.ShapeDtypeStruct((t, 3 * GDN_W), F32),
                   jax.ShapeDtypeStruct((t, GDN_W), F32),
                   jax.ShapeDtypeStruct((t, 128), F32),
                   jax.ShapeDtypeStruct((batch, RES, seq // RES, 3 * SWA_W), F32)],
        scratch_shapes=[pltpu.VMEM((3 * SWA_W // 128, TM, 128), F32)],
        compiler_params=_params("parallel"),
        name="mixer_proj",
    )(x, x, g, w_in, w_in, w_in, wb, cw)


def _softplus(x):
    return jnp.maximum(x, 0.0) + jnp.log(1.0 + jnp.exp(-jnp.abs(x)))


def _level_masks():
    row = lax.broadcasted_iota(jnp.int32, (CHUNK, CHUNK), 0)
    col = lax.broadcasted_iota(jnp.int32, (CHUNK, CHUNK), 1)
    sels = []
    for lb in range(CHUNK.bit_length() - 1):
        sels.append(((row >> (lb + 1)) == (col >> (lb + 1))) & (((row >> lb) & 1) == 1)
                    & (((col >> lb) & 1) == 0))
    return row, col, sels


def _gdn_kernel(x_ref, z_ref, abc_ref, alc_ref, dtc_ref, ng_ref, o_ref, st):
    nbatch = x_ref.shape[0]

    @pl.when(pl.program_id(0) == 0)
    def _():
        st[...] = jnp.zeros_like(st)

    row, col, sels = _level_masks()
    tril = col <= row
    strict = col < row
    eye = jnp.where(row == col, 1.0, 0.0).astype(F32)
    lower = jnp.where(tril, 1.0, 0.0).astype(F32)
    upper = jnp.where(row <= col, 1.0, 0.0).astype(F32)
    hi = lax.Precision.HIGHEST

    def prep(batches):
        items = []
        for b in batches:
            abc = abc_ref[b]
            g_col = -jnp.exp(alc_ref[...]) * _softplus(abc + dtc_ref[...])
            gc_col = jnp.dot(lower, g_col, precision=hi, preferred_element_type=F32)
            gc_row = lax.dot_general(g_col, upper, (((0,), (0,)), ((), ())), precision=hi,
                                     preferred_element_type=F32)
            beta_all = jax.nn.sigmoid(abc)
            for h in range(GDN_HEADS):
                items.append(dict(
                    b=b, h=h,
                    q=x_ref[b, :, h * GDN_DIM:(h + 1) * GDN_DIM],
                    k=x_ref[b, :, GDN_W + h * GDN_DIM:GDN_W + (h + 1) * GDN_DIM],
                    v=x_ref[b, :, 2 * GDN_W + h * GDN_DIM:2 * GDN_W + (h + 1) * GDN_DIM],
                    gcc=gc_col[:, h:h + 1], gcr=gc_row[h:h + 1, :],
                    gl=gc_col[CHUNK - 1:CHUNK, h:h + 1], beta=beta_all[:, 4 + h:5 + h]))
        return items

    def kq(items):
        for it in items:
            it["kb"] = it["k"] * it["beta"]
            it["kq"] = _dot_nt(jnp.concatenate([it["kb"], it["q"]], axis=0).astype(BF16),
                               it["k"].astype(BF16))

    def decay(items):
        for it in items:
            d = jnp.exp(jnp.where(tril, it["gcc"] - it["gcr"], NEG))
            kq_ = it.pop("kq")
            it["a"] = jnp.where(strict, kq_[:CHUNK] * d, 0.0)
            it["qk"] = jnp.where(tril, kq_[CHUNK:] * d, 0.0).astype(BF16)
            it["t"] = eye - jnp.where(sels[0], it["a"], 0.0)

    def level(sel):
        def run(items):
            for it in items:
                it["tb"] = it["t"].astype(BF16)
                it["p"] = _dot(jnp.where(sel, it["a"], 0.0).astype(BF16), it["tb"])
            for it in items:
                it["t"] = it["t"] - _dot(it.pop("tb"), it.pop("p").astype(BF16))
        return run

    def uw(items):
        for it in items:
            eg = jnp.exp(it["gcc"])
            rhs = jnp.concatenate([it["v"] * it["beta"], it["kb"] * eg], axis=1).astype(BF16)
            it["uw"] = _dot(it.pop("t").astype(BF16), rhs)
            it["qd"] = (it["q"] * eg).astype(BF16)
            it["kd"] = (it["k"] * jnp.exp(it["gl"] - it["gcc"])).astype(BF16)

    def ws(items):
        for it in items:
            it["s"] = st[it["b"] * GDN_HEADS + it["h"]]
            lhs = jnp.concatenate([it["uw"][:, GDN_DIM:].astype(BF16), it["qd"]], axis=0)
            it["ws"] = _dot(lhs, it["s"].astype(BF16))

    def out(items):
        for it in items:
            vb = (it["uw"][:, :GDN_DIM] - it["ws"][:CHUNK]).astype(BF16)
            o = it["ws"][CHUNK:] + _dot(it["qk"], vb)
            st[it["b"] * GDN_HEADS + it["h"]] = it["s"] * jnp.exp(it["gl"]) + _dot_tn(it["kd"], vb)
            hl = slice(it["h"] * GDN_DIM, (it["h"] + 1) * GDN_DIM)
            zz = z_ref[it["b"], :, hl]
            o = o * lax.rsqrt(jnp.mean(o * o, axis=-1, keepdims=True) + EPS)
            o_ref[it["b"], :, hl] = o * ng_ref[...] * (zz * jax.nn.sigmoid(zz))

    stages = [kq, decay] + [level(sel) for sel in sels[1:]] + [uw, ws, out]
    per = nbatch // GDN_GROUPS
    groups = [prep(range(g * per, (g + 1) * per)) for g in range(GDN_GROUPS)]
    for step in range(len(stages) + GDN_GROUPS - 1):
        for g, items in enumerate(groups):
            if 0 <= step - g < len(stages):
                stages[step - g](items)


def _gdn(qkva, z, abc, alc, dtc, ng):
    batch, seq, _ = qkva.shape
    blk = lambda w: pl.BlockSpec((batch, CHUNK, w), lambda c: (0, c, 0))
    const = lambda shape: pl.BlockSpec(shape, lambda c: (0, 0))
    return pl.pallas_call(
        _gdn_kernel,
        grid=(seq // CHUNK,),
        in_specs=[blk(3 * GDN_W), blk(GDN_W), blk(128),
                  const((1, 128)), const((1, 128)), const((1, GDN_DIM))],
        out_specs=blk(GDN_W),
        out_shape=jax.ShapeDtypeStruct((batch, seq, GDN_W), F32),
        scratch_shapes=[pltpu.VMEM((batch * GDN_HEADS, GDN_DIM, GDN_DIM), F32)],
        compiler_params=_params("arbitrary"),
        name="gdn",
    )(qkva, z, abc, alc, dtc, ng)


def _dil_bias(delta, dilation, slopes):
    valid = (delta >= 0) & (delta <= SWA_BLOCK)
    df = delta.astype(F32) * LOG2E
    return jnp.concatenate([jnp.where(valid, -(s * float(dilation)) * df, NEG) for s in slopes], axis=0)


def _dil_tables(hp, b1, b4, b16):
    def iota(shape, dim):
        return lax.broadcasted_iota(jnp.int32, shape, dim)

    def slope(head, shape):
        e = jnp.full(shape, 126, jnp.int32) - head
        return lax.bitcast_convert_type(lax.shift_left(e, jnp.full_like(e, 23)), F32)

    wide = (SWA_BLOCK, 2 * SWA_BLOCK)
    sl = [slope(2 * hp, wide), slope(2 * hp + 1, wide)]
    p, c = iota(wide, 0), iota(wide, 1)
    d = 16 * ((p & 7) - (c & 15)) + ((p >> 3) - (c >> 4))
    b1[0] = _dil_bias(d + 16 * 8, 1, sl)
    b1[1] = _dil_bias(d, 1, sl)
    d = 4 * ((p & 31) - (c & 63)) + ((p >> 5) - (c >> 6))
    b4[0] = _dil_bias(d + 4 * 32, 4, sl)
    b4[1] = _dil_bias(d, 4, sl)
    sq = (SWA_BLOCK, SWA_BLOCK)
    b16[...] = _dil_bias(iota(sq, 0) - iota(sq, 1), 16, [slope(2 * hp, sq), slope(2 * hp + 1, sq)])


def _attend(tiles, head_a):
    def score(tile):
        q, k, v, bias = tile
        qa = jnp.where(head_a, q, 0.0)
        qs = (jnp.concatenate([qa, q - qa], axis=0) * (SWA_DIM ** -0.5 * LOG2E)).astype(BF16)
        return _dot_nt(qs, k.astype(BF16)) + bias

    def prob(s):
        m = jnp.max(s, axis=-1, keepdims=True)
        return jnp.exp2(s - m).astype(BF16), m

    def out(tile, e, m):
        v = tile[2]
        v1 = jnp.concatenate([v.astype(BF16), jnp.ones(v.shape, BF16)], axis=1)
        ol = _dot(e, v1)
        top, bot = ol[:SWA_BLOCK], ol[SWA_BLOCK:]
        w = 2 * SWA_DIM
        return (jnp.where(head_a, top[:, :w], bot[:, :w]), jnp.where(head_a, top[:, w:], bot[:, w:]),
                jnp.where(head_a, m[:SWA_BLOCK], m[SWA_BLOCK:]))

    n = len(tiles)
    scores, probs, outs = {}, {}, []
    for j in range(n + 2 * DIL_SKEW):
        if j < n:
            scores[j] = score(tiles[j])
        if 0 <= j - DIL_SKEW < n:
            probs[j - DIL_SKEW] = prob(scores.pop(j - DIL_SKEW))
        if j - 2 * DIL_SKEW >= 0:
            outs.append(out(tiles[j - 2 * DIL_SKEW], *probs.pop(j - 2 * DIL_SKEW)))
    return outs


def _dil_kernel(q_ref, k_ref, v_ref, o_ref, b1, b4, b16, s1, s4, s16):
    hp = pl.program_id(0)

    @pl.when(pl.program_id(1) == 0)
    def _():
        _dil_tables(hp, b1, b4, b16)

    head_a = lax.broadcasted_iota(jnp.int32, (1, 2 * SWA_DIM), 1) < SWA_DIM
    u = DIL_UNROLL
    nblk = q_ref.shape[2] * RES // SWA_BLOCK

    def cat(ref, pieces):
        return jnp.concatenate([ref[0, r, rows, :] for r, rows in pieces], axis=0)

    def body1(it, carry):
        tiles, where = [], []
        for j in range(u):
            n = it * u + j
            qrows = pl.ds(pl.multiple_of(n * 8, 8), 8)
            krows = pl.ds(pl.multiple_of(jnp.maximum(n * 8 - 8, 0), 8), 16)
            bias = b1[0] if j else jnp.where(n == 0, b1[1], b1[0])
            tiles.append((cat(q_ref, [(r, qrows) for r in range(RES)]),
                          cat(k_ref, [(r, krows) for r in range(RES)]),
                          cat(v_ref, [(r, krows) for r in range(RES)]), bias))
            where.append(qrows)
        for qrows, stats in zip(where, _attend(tiles, head_a)):
            for i, val in enumerate(stats):
                for r in range(RES):
                    s1[i, r, qrows, :] = val[8 * r:8 * r + 8]
        return carry

    lax.fori_loop(0, nblk // u, body1, 0)

    def body4(it, carry):
        tiles, where = [], []
        for j in range(u):
            r4, n = it * (u // 4) + j // 4, j % 4
            qrows = slice(32 * n, 32 * n + 32)
            k0 = max(32 * n - 32, 0)
            krows = slice(k0, k0 + 64)
            grp = [r4 + 4 * g for g in range(4)]
            tiles.append((cat(q_ref, [(r, qrows) for r in grp]), cat(k_ref, [(r, krows) for r in grp]),
                          cat(v_ref, [(r, krows) for r in grp]), b4[0] if n else b4[1]))
            where.append((grp, qrows))
        for (grp, qrows), stats in zip(where, _attend(tiles, head_a)):
            for i, val in enumerate(stats):
                for g, r in enumerate(grp):
                    s4[i, r, qrows, :] = val[32 * g:32 * g + 32]
        return carry

    lax.fori_loop(0, nblk // u, body4, 0)

    def body16(it, carry):
        rs = [it * u + j for j in range(u)]
        tiles = [(q_ref[0, r], k_ref[0, r], v_ref[0, r], b16[...]) for r in rs]
        for r, stats in zip(rs, _attend(tiles, head_a)):
            for i, val in enumerate(stats):
                s16[i, r] = val
        return carry

    lax.fori_loop(0, RES // u, body16, 0)

    m = jnp.maximum(jnp.maximum(s1[2], s4[2]), s16[2])
    ws = [jnp.exp2(s[2] - m) for s in (s1, s4, s16)]
    num = sum(w * s[0] for w, s in zip(ws, (s1, s4, s16)))
    den = sum(w * s[1] for w, s in zip(ws, (s1, s4, s16)))
    o_ref[0] = num / den


def _dilated(qkvb):
    batch, _, sub, _ = qkvb.shape
    npair = SWA_HEADS // 2
    spec = lambda off: pl.BlockSpec((1, RES, sub, 2 * SWA_DIM), lambda p, b: (b, 0, 0, off + p))
    wide = (2 * SWA_BLOCK, 2 * SWA_BLOCK)
    res = pltpu.VMEM((3, RES, sub, 2 * SWA_DIM), F32)
    return pl.pallas_call(
        _dil_kernel,
        grid=(npair, batch),
        in_specs=[spec(0), spec(npair), spec(2 * npair)],
        out_specs=spec(0),
        out_shape=jax.ShapeDtypeStruct((batch, RES, sub, SWA_W), F32),
        scratch_shapes=[pltpu.VMEM((2,) + wide, F32), pltpu.VMEM((2,) + wide, F32),
                        pltpu.VMEM((2 * SWA_BLOCK, SWA_BLOCK), F32), res, res, res],
        compiler_params=_params("arbitrary", "arbitrary"),
        name="dilated_attn",
    )(qkvb, qkvb, qkvb)


def _memkv_kernel(m_ref, g_ref, wk_ref, wv_ref, k_ref, v_ref):
    mn = _rms(m_ref[...], g_ref[...]).astype(BF16)
    k_ref[...] = _dot(mn, wk_ref[...].astype(BF16)).astype(BF16)
    v_ref[...] = _dot(mn, wv_ref[...].astype(BF16)).astype(BF16)


def _memkv(mem, g, wk, wv):
    t = mem.shape[0]
    tm = min(TM, t)
    row = pl.BlockSpec((tm, D_MODEL), lambda i: (i, 0))
    return pl.pallas_call(
        _memkv_kernel,
        grid=(t // tm,),
        in_specs=[row, _resident((1, D_MODEL)), _resident((D_MODEL, D_MODEL)),
                  _resident((D_MODEL, D_MODEL))],
        out_specs=[row, row],
        out_shape=[jax.ShapeDtypeStruct((t, D_MODEL), BF16)] * 2,
        compiler_params=_params("parallel"),
        name="mem_kv",
    )(mem, g, wk, wv)


def _outmem_kernel(x_ref, oa_ref, ob_ref, woa_ref, wob_ref, gmix_ref, gpre_ref, wq_ref,
                   k_ref, v_ref, wo_ref, gpost_ref, o_ref, att, obn):
    tm = x_ref.shape[0]
    rs = tm // OUT_SUB
    subs = [slice(i * rs, (i + 1) * rs) for i in range(OUT_SUB)]
    woa, wob, wq, wo = (w[...].astype(BF16) for w in (woa_ref, wob_ref, wq_ref, wo_ref))
    mixes = [_dot(oa_ref[s, :].astype(BF16), woa) for s in subs]
    for l in range(SWA_W // 128):
        for r in range(RES):
            obn[l, pl.ds(r, tm // RES, stride=RES), :] = ob_ref[0, r, :, 128 * l:128 * l + 128]
    xs = []
    for s, mix in zip(subs, mixes):
        ob = jnp.concatenate([obn[l, s, :] for l in range(SWA_W // 128)], axis=1).astype(BF16)
        xs.append(x_ref[s, :] + _rms(mix + _dot(ob, wob), gmix_ref[...]))
    qs = [(_dot(_rms(x, gpre_ref[...]).astype(BF16), wq) * (MEM_DIM ** -0.5)).astype(BF16)
          for x in xs]
    for h in range(MEM_HEADS):
        hl = slice(h * MEM_DIM, (h + 1) * MEM_DIM)
        scores = [_dot_nt(q[:, hl], k_ref[:, hl]) for q in qs]
        for s, sc in zip(subs, scores):
            e = jnp.exp(sc - jnp.max(sc, axis=-1, keepdims=True))
            p = e / jnp.sum(e, axis=-1, keepdims=True)
            att[s, hl] = _dot(p.astype(BF16), v_ref[:, hl]).astype(BF16)
    cs = [_dot(att[s, :], wo) for s in subs]
    for s, x, c in zip(subs, xs, cs):
        o_ref[s, :] = x + _rms(c, gpost_ref[...])


def _outmem(x, oa, ob, woa, wob, gmix, gpre, wq, kmem, vmem, wo, gpost, seq, n_mem):
    t = x.shape[0]
    per_seq = seq // TM_OUT
    row = lambda w: pl.BlockSpec((TM_OUT, w), lambda i: (i, 0))
    kv = pl.BlockSpec((n_mem, D_MODEL), lambda i: (i // per_seq, 0))
    return pl.pallas_call(
        _outmem_kernel,
        grid=(t // TM_OUT,),
        in_specs=[row(D_MODEL), row(GDN_W),
                  pl.BlockSpec((1, RES, TM_OUT // RES, SWA_W), lambda i: (i // per_seq, 0, i % per_seq, 0)),
                  pl.BlockSpec((GDN_W, D_MODEL), lambda i: (0, 0), pipeline_mode=pl.Buffered(1)),
                  pl.BlockSpec((SWA_W, D_MODEL), lambda i: (GDN_W // SWA_W, 0), pipeline_mode=pl.Buffered(1)),
                  _resident((1, D_MODEL)),
                  _resident((1, D_MODEL)), _resident((D_MODEL, D_MODEL)), kv, kv,
                  _resident((D_MODEL, D_MODEL)), _resident((1, D_MODEL))],
        out_specs=row(D_MODEL),
        out_shape=jax.ShapeDtypeStruct((t, D_MODEL), F32),
        scratch_shapes=[pltpu.VMEM((TM_OUT, D_MODEL), BF16), pltpu.VMEM((SWA_W // 128, TM_OUT, 128), F32)],
        compiler_params=_params("parallel"),
        name="out_mem_attn",
    )(x, oa, ob, woa, wob, gmix, gpre, wq, kmem, vmem, wo, gpost)


def _layer(x, mem, p, batch, seq, n_mem):
    bf = lambda w: w.astype(BF16)
    row = lambda g: g.reshape(1, -1)
    x = _ffn(x, row(p["ffn1_pre_g"]), p["ffn1_w_gate"], p["ffn1_w_up"],
             p["ffn1_w_down"], row(p["ffn1_post_g"]))

    w_in = p["w_in"]
    o_b = 4 * GDN_W + 2 * GDN_HEADS
    cw = jnp.pad(p["gdn_conv_w"], ((0, 8 - GDN_CONV), (0, 0)))
    w_in = bf(w_in)
    qkva, z, abc, qkvb = _proj(x, row(p["mix_pre_g"]), w_in, w_in[:, o_b:], cw, batch, seq)

    pad_lane = lambda v: jnp.pad(v.reshape(1, -1), ((0, 0), (0, 128 - GDN_HEADS)))
    per_seq = lambda a: a.reshape(batch, seq, a.shape[-1])
    o_gdn = _gdn(per_seq(qkva), per_seq(z), per_seq(abc), pad_lane(p["gdn_a_log"]),
                 pad_lane(p["gdn_dt_bias"]), row(p["gdn_norm_g"])).reshape(batch * seq, GDN_W)
    o_dil = _dilated(qkvb)

    kmem, vmem = _memkv(mem, row(p["mem_kv_g"]), p["mem_wk"], p["mem_wv"])
    x = _outmem(x, o_gdn, o_dil, p["w_out"], p["w_out"], row(p["mix_post_g"]),
                row(p["mem_pre_g"]), p["mem_wq"], kmem, vmem, p["mem_wo"],
                row(p["mem_post_g"]), seq, n_mem)

    return _ffn(x, row(p["ffn2_pre_g"]), p["ffn2_w_gate"], p["ffn2_w_up"],
                p["ffn2_w_down"], row(p["ffn2_post_g"]))


_NAMES = ("ffn1_pre_g", "ffn1_w_gate", "ffn1_w_up", "ffn1_w_down", "ffn1_post_g",
          "mix_pre_g", "w_in", "gdn_conv_w", "gdn_a_log", "gdn_dt_bias", "gdn_norm_g", "w_out",
          "mix_post_g", "mem_pre_g", "mem_kv_g", "mem_wq", "mem_wk", "mem_wv", "mem_wo",
          "mem_post_g", "ffn2_pre_g", "ffn2_w_gate", "ffn2_w_up", "ffn2_w_down", "ffn2_post_g")


def kernel(x, mem, ffn1_pre_g, ffn1_w_gate, ffn1_w_up, ffn1_w_down, ffn1_post_g, mix_pre_g, w_in, gdn_conv_w, gdn_a_log, gdn_dt_bias, gdn_norm_g, w_out, mix_post_g, mem_pre_g, mem_kv_g, mem_wq, mem_wk, mem_wv, mem_wo, mem_post_g, ffn2_pre_g, ffn2_w_gate, ffn2_w_up, ffn2_w_down, ffn2_post_g):
    stacked = dict(zip(_NAMES, (ffn1_pre_g, ffn1_w_gate, ffn1_w_up, ffn1_w_down, ffn1_post_g,
                                mix_pre_g, w_in, gdn_conv_w, gdn_a_log, gdn_dt_bias, gdn_norm_g, w_out,
                                mix_post_g, mem_pre_g, mem_kv_g, mem_wq, mem_wk, mem_wv, mem_wo,
                                mem_post_g, ffn2_pre_g, ffn2_w_gate, ffn2_w_up, ffn2_w_down,
                                ffn2_post_g)))
    batch, seq, _ = x.shape
    n_mem = mem.shape[1]
    xf = x.reshape(batch * seq, D_MODEL)
    memf = mem.reshape(batch * n_mem, D_MODEL)
    for l in range(ffn1_pre_g.shape[0]):
        xf = _layer(xf, memf, {k: v[l] for k, v in stacked.items()}, batch, seq, n_mem)
    return xf.reshape(batch, seq, D_MODEL)
```
